```python
import functools
import jax
import jax.numpy as jnp
from jax import lax
import numpy as np

D_MODEL = 1024
BATCH = 2
SEQ = 16384
DEPTH = 1
DEC_BATCH = 128
DEC_SEQ = 4
PAST_LEN = 8192
PAGE_SIZE = 128

N_HEADS = 8
HEAD_DIM = 64
D_ATT = N_HEADS * HEAD_DIM
D_CONV = D_MODEL // 2
CONV_WIDTH = 31
N_EXPERTS = 32
TOP_K = 4
D_EXPERT = D_MODEL
SWIGLU_LIMIT = 7.0
SWIGLU_ALPHA = 1.702
Q_BLOCK = 128
MOE_BLOCK = 128
RMS_EPS = 1e-6
LN_EPS = 1e-5
SB_BIAS_INIT = -7.0
D_IN = 3 * D_ATT + 2 * D_CONV + 2 * D_MODEL
IN_SPLITS = (D_ATT, 2 * D_ATT, 3 * D_ATT, 3 * D_ATT + D_CONV,
             3 * D_ATT + 2 * D_CONV, 3 * D_ATT + 2 * D_CONV + D_MODEL)

kernel_name = "stickbreak_conformer_moe_hybrid_step"


def rms_norm(x, g):
    xf = x.astype(jnp.float32)
    y = xf * lax.rsqrt(jnp.mean(xf * xf, axis=-1, keepdims=True) + RMS_EPS)
    return (y * g.astype(jnp.float32)).astype(x.dtype)


def layer_norm(x, g, b):
    xf = x.astype(jnp.float32)
    mu = jnp.mean(xf, axis=-1, keepdims=True)
    xc = xf - mu
    y = xc * lax.rsqrt(jnp.mean(xc * xc, axis=-1, keepdims=True) + LN_EPS)
    return (y * g.astype(jnp.float32) + b.astype(jnp.float32)).astype(x.dtype)


def adaln(c, w_mod, b_mod):
    m = jax.nn.silu(c) @ w_mod + b_mod
    return jnp.split(m[:, None, :], 6, axis=-1)


def stick_breaking_weights(z, valid):
    z = z.astype(jnp.float32)
    log_1mb = jnp.where(valid, jax.nn.log_sigmoid(-z), 0.0)
    suffix = lax.cumsum(log_1mb, axis=z.ndim - 1, reverse=True) - log_1mb
    return jnp.where(valid, jnp.exp(jax.nn.log_sigmoid(z) + suffix), 0.0)


def sb_attention_prompt(q, k, v, bias):
    n, s = q.shape[:2]
    nb = s // Q_BLOCK
    qb = q.reshape(n, nb, Q_BLOCK, N_HEADS, HEAD_DIM).transpose(1, 0, 2, 3, 4)
    key_pos = jnp.arange(s)
    scale = HEAD_DIM ** -0.5
    b_h = bias.astype(jnp.float32)[None, :, None, None]

    def block(args):
        q_blk, i = args
        q_pos = i * Q_BLOCK + jnp.arange(Q_BLOCK)
        z = jnp.einsum("bqhd,bkhd->bhqk", q_blk, k, preferred_element_type=jnp.float32) * scale + b_h
        a = stick_breaking_weights(z, key_pos[None, :] < q_pos[:, None])
        return jnp.einsum("bhqk,bkhd->bqhd", a.astype(v.dtype), v)

    o = lax.map(block, (qb, jnp.arange(nb)))
    return o.transpose(1, 0, 2, 3, 4).reshape(n, s, N_HEADS, HEAD_DIM)


def sb_attention_sample(q, k_new, v_new, cache_k, cache_v, page_table, layer, bias):
    nb, t = q.shape[:2]
    past = page_table.shape[1] * PAGE_SIZE
    k_past = cache_k[layer][page_table].reshape(nb, past, N_HEADS, HEAD_DIM)
    v_past = cache_v[layer][page_table].reshape(nb, past, N_HEADS, HEAD_DIM)
    scale = HEAD_DIM ** -0.5
    b_h = bias.astype(jnp.float32)[None, :, None, None]
    z = jnp.concatenate([
        jnp.einsum("bqhd,bkhd->bhqk", q, k_past, preferred_element_type=jnp.float32),
        jnp.einsum("bqhd,bkhd->bhqk", q, k_new, preferred_element_type=jnp.float32)], axis=-1) * scale + b_h
    q_pos = past + jnp.arange(t)
    key_pos = jnp.arange(past + t)
    a = stick_breaking_weights(z, key_pos[None, :] < q_pos[:, None]).astype(v_new.dtype)
    return (jnp.einsum("bhqk,bkhd->bqhd", a[..., :past], v_past)
            + jnp.einsum("bhqk,bkhd->bqhd", a[..., past:], v_new))


def conv_tail(u_ext, w_dw, b_dw, ln_g, ln_b, w_pw, b_pw):
    y = lax.conv_general_dilated(u_ext, w_dw[:, None, :], window_strides=(1,), padding="VALID",
                                 dimension_numbers=("NWC", "WIO", "NWC"),
                                 feature_group_count=D_CONV) + b_dw
    y = jax.nn.silu(layer_norm(y, ln_g, ln_b))
    return y @ w_pw + b_pw


def moe_ffn(h, w_router, b_router, w_gu, b_gu, w_down, b_down):
    n, t, d = h.shape
    n_tok = n * t
    xt = h.reshape(n_tok, d)
    logits = (xt @ w_router + b_router).astype(jnp.float32)
    top_v, top_e = lax.top_k(logits, TOP_K)
    gates = jax.nn.softmax(top_v, axis=-1)
    flat_e = top_e.reshape(-1).astype(jnp.int32)
    flat_tok = jnp.repeat(jnp.arange(n_tok, dtype=jnp.int32), TOP_K)
    flat_w = gates.reshape(-1)
    order = jnp.argsort(flat_e)
    sorted_e = flat_e[order]
    counts = jnp.bincount(flat_e, length=N_EXPERTS).astype(jnp.int32)
    padded = (counts + MOE_BLOCK - 1) // MOE_BLOCK * MOE_BLOCK
    start = jnp.cumsum(counts) - counts
    pend = jnp.cumsum(padded)
    pstart = pend - padded
    dest = pstart[sorted_e] + jnp.arange(n_tok * TOP_K, dtype=jnp.int32) - start[sorted_e]
    n_blocks = (n_tok * TOP_K + MOE_BLOCK - 1) // MOE_BLOCK + N_EXPERTS
    n_rows = n_blocks * MOE_BLOCK
    row_tok = jnp.full((n_rows,), n_tok, jnp.int32).at[dest].set(flat_tok[order])
    row_w = jnp.zeros((n_rows,), jnp.float32).at[dest].set(flat_w[order])
    block_e = jnp.minimum(jnp.searchsorted(pend, jnp.arange(n_blocks, dtype=jnp.int32) * MOE_BLOCK,
                                           side="right"), N_EXPERTS - 1)
    x_pad = jnp.concatenate([xt, jnp.zeros((1, d), xt.dtype)], axis=0)
    xb = x_pad[row_tok].reshape(n_blocks, MOE_BLOCK, d)

    def expert_block(args):
        xblk, e = args
        gu = xblk @ w_gu[e] + b_gu[e]
        g, u = jnp.split(gu, 2, axis=-1)
        g = jnp.minimum(g, SWIGLU_LIMIT)
        u = jnp.clip(u, -SWIGLU_LIMIT, SWIGLU_LIMIT)
        return ((u + 1) * g * jax.nn.sigmoid(SWIGLU_ALPHA * g)) @ w_down[e] + b_down[e]

    yb = lax.map(expert_block, (xb, block_e)).reshape(n_rows, d)
    y = jnp.zeros((n_tok + 1, d), jnp.float32).at[row_tok].add(yb.astype(jnp.float32) * row_w[:, None])
    return y[:n_tok].astype(h.dtype).reshape(n, t, d)


def trunk_layer(x, c, attend, conv_prefix, w_mod, b_mod, g_pre_mix, g_post_mix, w_in, w_att_out,
                w_dw, b_dw, ln_conv_g, ln_conv_b, w_conv_out, b_conv_out, w_out,
                g_pre_ffn, g_post_ffn, w_router, b_router, w_gu, b_gu, w_down, b_down):
    n, t = x.shape[:2]
    sh1, sc1, gt1, sh2, sc2, gt2 = adaln(c, w_mod, b_mod)
    h = rms_norm(x, g_pre_mix) * (1 + sc1) + sh1
    q, k, v, ca, cb, ga, gc = jnp.split(h @ w_in, IN_SPLITS, axis=-1)
    q = q.reshape(n, t, N_HEADS, HEAD_DIM)
    k = k.reshape(n, t, N_HEADS, HEAD_DIM)
    v = v.reshape(n, t, N_HEADS, HEAD_DIM)
    y_att = attend(q, k, v).reshape(n, t, D_ATT) @ w_att_out
    u_ext = jnp.concatenate([conv_prefix.astype(ca.dtype), ca * jax.nn.sigmoid(cb)], axis=1)
    y_conv = conv_tail(u_ext, w_dw, b_dw, ln_conv_g, ln_conv_b, w_conv_out, b_conv_out)
    mixed = (jax.nn.sigmoid(ga) * y_att + jax.nn.sigmoid(gc) * y_conv) @ w_out
    x = x + gt1 * rms_norm(mixed, g_post_mix)
    h2 = rms_norm(x, g_pre_ffn) * (1 + sc2) + sh2
    x = x + gt2 * rms_norm(moe_ffn(h2, w_router, b_router, w_gu, b_gu, w_down, b_down), g_post_ffn)
    return x, k, v, u_ext[:, -(CONV_WIDTH - 1):]


def setup_inputs(seed: int = 0) -> dict:
    key = jax.random.key(seed)
    ks = jax.random.split(key, 32)
    n_pages = PAST_LEN // PAGE_SIZE
    n_used = DEC_BATCH * n_pages
    n_pool = (n_used * 5 + 3) // 4
    f32 = jnp.float32

    def nrm(k, shape, s):
        return jax.random.normal(k, shape, f32) * s

    def gain(k, shape):
        return 1.0 + 0.05 * jax.random.normal(k, shape, f32)

    page_table = jax.random.permutation(ks[5], n_pool)[:n_used].reshape(DEC_BATCH, n_pages).astype(jnp.int32)
    return {
        "x_prompt": nrm(ks[0], (BATCH, SEQ, D_MODEL), 1.0),
        "x_sample": nrm(ks[1], (DEC_BATCH, DEC_SEQ, D_MODEL), 1.0),
        "cache_k": nrm(ks[2], (DEPTH, n_pool, PAGE_SIZE, N_HEADS, HEAD_DIM), 1.0),
        "cache_v": nrm(ks[3], (DEPTH, n_pool, PAGE_SIZE, N_HEADS, HEAD_DIM), 1.0),
        "state_conv": nrm(ks[4], (DEPTH, DEC_BATCH, CONV_WIDTH - 1, D_CONV), 0.5),
        "page_table": page_table,
        "c_prompt": nrm(ks[6], (BATCH, D_MODEL), 1.0),
        "c_sample": nrm(ks[7], (DEC_BATCH, D_MODEL), 1.0),
        "w_mod": nrm(ks[8], (DEPTH, D_MODEL, 6 * D_MODEL), D_MODEL ** -0.5),
        "b_mod": nrm(ks[9], (DEPTH, 6 * D_MODEL), 0.02),
        "g_pre_mix": gain(ks[10], (DEPTH, D_MODEL)),
        "g_post_mix": gain(ks[11], (DEPTH, D_MODEL)),
        "w_in": nrm(ks[12], (DEPTH, D_MODEL, D_IN), D_MODEL ** -0.5),
        "b_sb": SB_BIAS_INIT + nrm(ks[29], (DEPTH, N_HEADS), 0.1),
        "w_att_out": nrm(ks[13], (DEPTH, D_ATT, D_MODEL), D_ATT ** -0.5),
        "w_dw": nrm(ks[14], (DEPTH, CONV_WIDTH, D_CONV), CONV_WIDTH ** -0.5),
        "b_dw": nrm(ks[15], (DEPTH, D_CONV), 0.02),
        "ln_conv_g": gain(ks[16], (DEPTH, D_CONV)),
        "ln_conv_b": nrm(ks[17], (DEPTH, D_CONV), 0.02),
        "w_conv_out": nrm(ks[18], (DEPTH, D_CONV, D_MODEL), D_CONV ** -0.5),
        "b_conv_out": nrm(ks[19], (DEPTH, D_MODEL), 0.02),
        "w_out": nrm(ks[20], (DEPTH, D_MODEL, D_MODEL), D_MODEL ** -0.5),
        "g_pre_ffn": gain(ks[21], (DEPTH, D_MODEL)),
        "g_post_ffn": gain(ks[22], (DEPTH, D_MODEL)),
        "w_router": nrm(ks[23], (DEPTH, D_MODEL, N_EXPERTS), D_MODEL ** -0.5),
        "b_router": nrm(ks[24], (DEPTH, N_EXPERTS), 0.01),
        "w_gu": nrm(ks[25], (DEPTH, N_EXPERTS, D_MODEL, 2 * D_EXPERT), D_MODEL ** -0.5),
        "b_gu": nrm(ks[26], (DEPTH, N_EXPERTS, 2 * D_EXPERT), 0.02),
        "w_down": nrm(ks[27], (DEPTH, N_EXPERTS, D_EXPERT, D_MODEL), D_EXPERT ** -0.5),
        "b_down": nrm(ks[28], (DEPTH, N_EXPERTS, D_MODEL), 0.02),
    }


def reference(x_prompt, x_sample, cache_k, cache_v, state_conv, page_table, c_prompt, c_sample,
              w_mod, b_mod, g_pre_mix, g_post_mix, w_in, b_sb, w_att_out, w_dw, b_dw, ln_conv_g, ln_conv_b,
              w_conv_out, b_conv_out, w_out, g_pre_ffn, g_post_ffn, w_router, b_router,
              w_gu, b_gu, w_down, b_down):
    xp, xs = x_prompt, x_sample
    kps, vps, cps, kss, vss, css = [], [], [], [], [], []
    for l in range(DEPTH):
        lw = (w_mod[l], b_mod[l], g_pre_mix[l], g_post_mix[l], w_in[l], w_att_out[l],
              w_dw[l], b_dw[l], ln_conv_g[l], ln_conv_b[l], w_conv_out[l], b_conv_out[l], w_out[l],
              g_pre_ffn[l], g_post_ffn[l], w_router[l], b_router[l], w_gu[l], b_gu[l],
              w_down[l], b_down[l])
        prompt_prefix = jnp.zeros((xp.shape[0], CONV_WIDTH - 1, D_CONV), xp.dtype)
        attend_prompt = functools.partial(sb_attention_prompt, bias=b_sb[l])
        xp, kp, vp, cp = trunk_layer(xp, c_prompt, attend_prompt, prompt_prefix, *lw)
        attend_sample = functools.partial(sb_attention_sample, cache_k=cache_k, cache_v=cache_v,
                                          page_table=page_table, layer=l, bias=b_sb[l])
        xs, k_s, v_s, c_s = trunk_layer(xs, c_sample, attend_sample, state_conv[l], *lw)
        kps.append(kp)
        vps.append(vp)
        cps.append(cp)
        kss.append(k_s)
        vss.append(v_s)
        css.append(c_s)
    return (xp, xs, jnp.stack(kps), jnp.stack(vps), jnp.stack(cps),
            jnp.stack(kss), jnp.stack(vss), jnp.stack(css))
```

```python
import functools

import jax
import jax.numpy as jnp
from jax import lax
from jax.experimental import pallas as pl
from jax.experimental.pallas import tpu as pltpu

F32 = jnp.float32
BF16 = jnp.bfloat16

N_HEADS = 8
TOP_K = 4
SWIGLU_LIMIT = 7.0
SWIGLU_ALPHA = 1.702
RMS_EPS = 1e-6
LN_EPS = 1e-5

VMEM_LIMIT_BYTES = 48 * 1024 * 1024
HALO = 32
TOKEN_TILE = 256
ATT_TILE = 256
MOE_TILE = 512
PAGES_PER_STEP = 4


def _params(n_axes):
    return pltpu.CompilerParams(dimension_semantics=("arbitrary",) * n_axes,
                                vmem_limit_bytes=VMEM_LIMIT_BYTES)


def _dot(a, b):
    return jnp.dot(a, b, preferred_element_type=F32)


def _dot_nt(a, b):
    return lax.dot_general(a, b, (((1,), (1,)), ((), ())), preferred_element_type=F32)


def _rms(x, g):
    return x * lax.rsqrt(jnp.mean(x * x, axis=-1, keepdims=True) + RMS_EPS) * g


def _sigmoid(x):
    return 1.0 / (1.0 + jnp.exp(-x))


def _mod_body(c_ref, w_ref, b_ref, o_ref):
    c = c_ref[...]
    s = (c * _sigmoid(c)).astype(BF16)
    o_ref[...] = _dot(s, w_ref[...].astype(BF16)) + b_ref[...]


def _modulation(c, w_mod, b_mod):
    n, d = c.shape
    n_out = w_mod.shape[1]
    tn = n_out // 6
    return pl.pallas_call(
        _mod_body,
        grid=(n_out // tn,),
        in_specs=[pl.BlockSpec((n, d), lambda j: (0, 0)),
                  pl.BlockSpec((d, tn), lambda j: (0, j)),
                  pl.BlockSpec((1, tn), lambda j: (0, j))],
        out_specs=pl.BlockSpec((n, tn), lambda j: (0, j)),
        out_shape=jax.ShapeDtypeStruct((n, n_out), F32),
        compiler_params=_params(1),
        name="mod",
    )(c, w_mod, b_mod.reshape(1, n_out))


def _inproj_body(x_ref, sc_ref, sh_ref, g_ref, w_ref,
                 q_ref, k_ref, v_ref, kb_ref, vb_ref, u_ref, ga_ref, gc_ref, *, d_att, d_conv, q_scale):
    d_model = x_ref.shape[-1]
    h = _rms(x_ref[...], g_ref[...]) * (1.0 + sc_ref[...]) + sh_ref[...]
    hb = h.astype(BF16)

    def proj(lo, width):
        return _dot(hb, w_ref[:, lo:lo + width])

    q_ref[...] = (proj(0, d_att) * q_scale).astype(BF16)
    k = proj(d_att, d_att)
    k_ref[...] = k
    kb_ref[...] = k.astype(BF16)
    v = proj(2 * d_att, d_att)
    v_ref[...] = v
    vb_ref[...] = v.astype(BF16)
    ca = proj(3 * d_att, d_conv)
    cb = proj(3 * d_att + d_conv, d_conv)
    u_ref[...] = ca * _sigmoid(cb)
    ga_ref[...] = _sigmoid(proj(3 * d_att + 2 * d_conv, d_model))
    gc_ref[...] = _sigmoid(proj(3 * d_att + 2 * d_conv + d_model, d_model))


def _mod_spec(per_token, tm, d, rows_per_seq):
    if per_token:
        return pl.BlockSpec((tm, d), lambda i: (i, 0))
    tiles_per_seq = rows_per_seq // tm
    return pl.BlockSpec((None, 1, d), lambda i: (i // tiles_per_seq, 0, 0))


def _inproj(x, sc, sh, g, w_in, *, d_att, d_conv, per_token, rows_per_seq):
    rows, d = x.shape
    tm = min(TOKEN_TILE, rows)
    head_dim = d_att // N_HEADS
    row_spec = lambda width: pl.BlockSpec((tm, width), lambda i: (i, 0))
    mod_spec = _mod_spec(per_token, tm, d, rows_per_seq)
    widths = (d_att, d_att, d_att, d_att, d_att, d_conv, d, d)
    dtypes = (BF16, F32, F32, BF16, BF16, F32, F32, F32)
    return pl.pallas_call(
        functools.partial(_inproj_body, d_att=d_att, d_conv=d_conv, q_scale=head_dim ** -0.5),
        grid=(rows // tm,),
        in_specs=[row_spec(d), mod_spec, mod_spec,
                  pl.BlockSpec((1, d), lambda i: (0, 0)),
                  pl.BlockSpec(w_in.shape, lambda i: (0, 0))],
        out_specs=[row_spec(w) for w in widths],
        out_shape=[jax.ShapeDtypeStruct((rows, w), t) for w, t in zip(widths, dtypes)],
        compiler_params=_params(1),
        name="inproj",
    )(x, sc, sh, g.reshape(1, d), w_in)


def _sb_tile(z, tri, valid):
    sp = jnp.maximum(z, 0.0) + jnp.log(1.0 + jnp.exp(-jnp.abs(z)))
    log_beta = z - sp
    if valid is not None:
        sp = jnp.where(valid, sp, 0.0)
    hi = sp.astype(BF16)
    lo = (sp - hi.astype(F32)).astype(BF16)
    right = _dot(hi, tri) + _dot(lo, tri)
    p = jnp.exp(log_beta - right)
    if valid is not None:
        p = jnp.where(valid, p, 0.0)
    total = right[:, 0:1] + sp[:, 0:1]
    return p.astype(BF16), total


def _sb_prompt_body(bias_ref, q_ref, k_ref, v_ref, o_ref, acc_ref, used_ref, *, tile, head_dim):
    hp = pl.program_id(1)
    qi = pl.program_id(2)
    lane = lax.broadcasted_iota(jnp.int32, (1, 2 * head_dim), 1)
    row = lax.broadcasted_iota(jnp.int32, (tile, tile), 0)
    col = lax.broadcasted_iota(jnp.int32, (tile, tile), 1)
    tri = (row > col).astype(BF16)
    causal = col < row
    q = q_ref[0]
    qh = [jnp.where((lane >= h * head_dim) & (lane < (h + 1) * head_dim), q, jnp.zeros_like(q))
          for h in range(2)]

    def head_tile(h, off, valid):
        kblk = k_ref[0, pl.ds(off, tile), :]
        vblk = v_ref[0, pl.ds(off, tile), :]
        z = _dot_nt(qh[h], kblk) + bias_ref[2 * hp + h]
        p, total = _sb_tile(z, tri, valid)
        return _dot(p, vblk), total

    for h in range(2):
        pv, total = head_tile(h, pl.multiple_of(qi * tile, tile), causal)
        acc_ref[h] = pv
        used_ref[h] = total

    def step(j, carry):
        off = pl.multiple_of((qi - 1 - j) * tile, tile)
        for h in range(2):
            pv, total = head_tile(h, off, None)
            used = used_ref[h]
            acc_ref[h] += jnp.exp(-used) * pv
            used_ref[h] = used + total
        return carry

    lax.fori_loop(0, qi, step, 0)
    o_ref[0] = jnp.where(lane < head_dim, acc_ref[0], acc_ref[1]).astype(o_ref.dtype)


def _sb_attention_prompt(q, k, v, bias):
    n, s, d_att = q.shape
    head_dim = d_att // N_HEADS
    tile = min(ATT_TILE, s)
    pair = 2 * head_dim
    grid_spec = pltpu.PrefetchScalarGridSpec(
        num_scalar_prefetch=1,
        grid=(n, N_HEADS // 2, s // tile),
        in_specs=[pl.BlockSpec((1, tile, pair), lambda b, hp, qi, bias: (b, qi, hp)),
                  pl.BlockSpec((1, s, pair), lambda b, hp, qi, bias: (b, 0, hp)),
                  pl.BlockSpec((1, s, pair), lambda b, hp, qi, bias: (b, 0, hp))],
        out_specs=pl.BlockSpec((1, tile, pair), lambda b, hp, qi, bias: (b, qi, hp)),
        scratch_shapes=[pltpu.VMEM((2, tile, pair), F32), pltpu.VMEM((2, tile, 1), F32)])
    return pl.pallas_call(
        functools.partial(_sb_prompt_body, tile=tile, head_dim=head_dim),
        grid_spec=grid_spec,
        out_shape=jax.ShapeDtypeStruct((n, s, d_att), BF16),
        compiler_params=_params(3),
        name="sb_prompt",
    )(bias, q, k, v)


def _sb_sample_body(pt_ref, qbd_ref, bias_ref, knew_ref, vnew_ref, *rest, page, t_new, n_steps):
    pages = rest[:2 * PAGES_PER_STEP]
    o_ref, acc_ref, used_ref = rest[2 * PAGES_PER_STEP:]
    g = pl.program_id(1)
    rows, d_att = acc_ref.shape
    head_dim = d_att // N_HEADS
    row = lax.broadcasted_iota(jnp.int32, (page, page), 0)
    col = lax.broadcasted_iota(jnp.int32, (page, page), 1)
    tri = (row > col).astype(BF16)
    qbd = qbd_ref[0]
    bias = bias_ref[...]

    def key_tile(k_f32, v_f32, valid):
        z = _dot_nt(qbd, k_f32.astype(BF16)) + bias
        p, total = _sb_tile(z, tri, valid)
        return _dot(p, v_f32.astype(BF16)), total

    @pl.when(g == 0)
    def _():
        qidx = lax.broadcasted_iota(jnp.int32, (rows, page), 0) // N_HEADS
        kidx = lax.broadcasted_iota(jnp.int32, (rows, page), 1)
        pv, total = key_tile(knew_ref[0], vnew_ref[0], kidx < qidx)
        acc_ref[...] = pv
        used_ref[...] = total

    for i in range(PAGES_PER_STEP):
        pv, total = key_tile(pages[2 * i][0], pages[2 * i + 1][0], None)
        used = used_ref[...]
        acc_ref[...] += jnp.exp(-used) * pv
        used_ref[...] = used + total

    @pl.when(g == n_steps - 1)
    def _():
        head = lax.broadcasted_iota(jnp.int32, (rows, d_att), 0) % N_HEADS
        lane_head = lax.broadcasted_iota(jnp.int32, (rows, d_att), 1) // head_dim
        own = jnp.where(head == lane_head, acc_ref[...], 0.0)
        o_ref[0] = jnp.sum(own.reshape(t_new, N_HEADS, d_att), axis=1)


def _sb_attention_sample(q, k_new, v_new, cache_k, cache_v, page_table, bias):
    nb, t_new, d_att = q.shape
    head_dim = d_att // N_HEADS
    page = cache_k.shape[1]
    n_pages = page_table.shape[1]
    n_steps = n_pages // PAGES_PER_STEP
    rows = t_new * N_HEADS
    head_mask = (jnp.arange(d_att)[None, :] // head_dim == jnp.arange(N_HEADS)[:, None]).astype(q.dtype)
    qbd = (q[:, :, None, :] * head_mask[None, None]).reshape(nb, rows, d_att)
    bias_rows = jnp.broadcast_to(jnp.tile(bias.astype(F32), t_new)[:, None], (rows, page))
    pad = ((0, 0), (0, page - t_new), (0, 0))
    k_new_p = jnp.pad(k_new, pad)
    v_new_p = jnp.pad(v_new, pad)

    def page_spec(i):
        def index(b, g, pt):
            return (pt[b, n_pages - 1 - (g * PAGES_PER_STEP + i)], 0, 0)
        return pl.BlockSpec((1, page, d_att), index)

    page_specs, page_args = [], []
    for i in range(PAGES_PER_STEP):
        page_specs += [page_spec(i), page_spec(i)]
        page_args += [cache_k, cache_v]
    seq_spec = lambda r: pl.BlockSpec((1, r, d_att), lambda b, g, pt: (b, 0, 0))
    grid_spec = pltpu.PrefetchScalarGridSpec(
        num_scalar_prefetch=1,
        grid=(nb, n_steps),
        in_specs=[seq_spec(rows), pl.BlockSpec((rows, page), lambda b, g, pt: (0, 0)),
                  seq_spec(page), seq_spec(page)] + page_specs,
        out_specs=seq_spec(t_new),
        scratch_shapes=[pltpu.VMEM((rows, d_att), F32), pltpu.VMEM((rows, 1), F32)])
    return pl.pallas_call(
        functools.partial(_sb_sample_body, page=page, t_new=t_new, n_steps=n_steps),
        grid_spec=grid_spec,
        out_shape=jax.ShapeDtypeStruct((nb, t_new, d_att), F32),
        compiler_params=_params(2),
        name="sb_sample",
    )(page_table, qbd, bias_rows, k_new_p, v_new_p, *page_args)


def _conv_body(prev_ref, cur_ref, w_ref, b_ref, o_ref, ext_ref, *, width, tiles_per_seq, chunk):
    bs, tm, _ = cur_ref.shape
    prev = prev_ref[...]
    if tiles_per_seq is not None:
        prev = jnp.where(pl.program_id(0) % tiles_per_seq == 0, 0.0, prev)
    ext_ref[:, 0:HALO, :] = prev
    ext_ref[:, HALO:, :] = cur_ref[...]
    first = HALO - (width - 1)
    for b in range(bs):
        for r0 in range(0, tm, chunk):
            acc = ext_ref[b, first + r0:first + r0 + chunk, :] * w_ref[0:1, :]
            for w in range(1, width):
                acc += ext_ref[b, first + r0 + w:first + r0 + w + chunk, :] * w_ref[w:w + 1, :]
            o_ref[b, r0:r0 + chunk, :] = acc + b_ref[...]


def _depthwise_conv(prev, cur, w_dw, b_dw, *, bs, tm, tiles_per_seq):
    n, rows, c = cur.shape
    width = w_dw.shape[0]
    w_pad = jnp.pad(w_dw, ((0, HALO - width), (0, 0)))
    if tiles_per_seq is None:
        grid = (n // bs,)
        prev_index = lambda i: (i, 0, 0)
        cur_index = lambda i: (i, 0, 0)
    else:
        grid = (n * tiles_per_seq,)
        halo_per_tile = tm // HALO
        prev_index = lambda i: (jnp.maximum(i * halo_per_tile - 1, 0), 0, 0)
        cur_index = lambda i: (i // tiles_per_seq, i % tiles_per_seq, 0)
    return pl.pallas_call(
        functools.partial(_conv_body, width=width, tiles_per_seq=tiles_per_seq, chunk=min(64, tm)),
        grid=grid,
        in_specs=[pl.BlockSpec((bs, HALO, c), prev_index),
                  pl.BlockSpec((bs, tm, c), cur_index),
                  pl.BlockSpec((HALO, c), lambda i: (0, 0)),
                  pl.BlockSpec((1, c), lambda i: (0, 0))],
        out_specs=pl.BlockSpec((bs, tm, c), cur_index),
        out_shape=jax.ShapeDtypeStruct((n, rows, c), F32),
        scratch_shapes=[pltpu.VMEM((bs, HALO + tm, c), F32)],
        compiler_params=_params(1),
        name="conv",
    )(prev, cur, w_pad, b_dw.reshape(1, c))


def _merge_body(att_ref, dw_ref, ga_ref, gc_ref, x_ref, gt1_ref, sc2_ref, sh2_ref,
                w_att_ref, ln_g_ref, ln_b_ref, w_conv_ref, b_conv_ref, w_out_ref,
                g_post_ref, g_pre_ref, w_router_ref, b_router_ref,
                x1_ref, h2_ref, logit_ref):
    y_att = _dot(att_ref[...].astype(BF16), w_att_ref[...])
    dw = dw_ref[...]
    mu = jnp.mean(dw, axis=-1, keepdims=True)
    xc = dw - mu
    ln = xc * lax.rsqrt(jnp.mean(xc * xc, axis=-1, keepdims=True) + LN_EPS) * ln_g_ref[...] + ln_b_ref[...]
    act = ln * _sigmoid(ln)
    y_conv = _dot(act.astype(BF16), w_conv_ref[...]) + b_conv_ref[...]
    mixed = ga_ref[...] * y_att + gc_ref[...] * y_conv
    out = _dot(mixed.astype(BF16), w_out_ref[...])
    x1 = x_ref[...] + gt1_ref[...] * _rms(out, g_post_ref[...])
    x1_ref[...] = x1
    h2 = (_rms(x1, g_pre_ref[...]) * (1.0 + sc2_ref[...]) + sh2_ref[...]).astype(BF16)
    h2_ref[...] = h2
    logit_ref[...] = _dot(h2, w_router_ref[...]) + b_router_ref[...]


def _merge(att, dw, ga, gc, x, gt1, sc2, sh2, weights, *, per_token, rows_per_seq):
    rows, d = x.shape
    tm = min(TOKEN_TILE, rows)
    n_exp = weights[-2].shape[1]
    row_spec = lambda a: pl.BlockSpec((tm, a.shape[1]), lambda i: (i, 0))
    mod_spec = _mod_spec(per_token, tm, d, rows_per_seq)
    full_spec = lambda a: pl.BlockSpec(a.shape, lambda i: (0, 0))
    out_widths = (d, d, n_exp)
    out_dtypes = (F32, BF16, F32)
    return pl.pallas_call(
        _merge_body,
        grid=(rows // tm,),
        in_specs=[row_spec(a) for a in (att, dw, ga, gc, x)] + [mod_spec] * 3 + [full_spec(w) for w in weights],
        out_specs=[pl.BlockSpec((tm, w), lambda i: (i, 0)) for w in out_widths],
        out_shape=[jax.ShapeDtypeStruct((rows, w), t) for w, t in zip(out_widths, out_dtypes)],
        compiler_params=_params(1),
        name="merge",
    )(att, dw, ga, gc, x, gt1, sc2, sh2, *weights)


def _moe_body(tile_expert_ref, n_tiles_ref, x_ref, w_gu_ref, b_gu_ref, w_down_ref, b_down_ref, o_ref, *, chunk):
    d_expert = w_down_ref.shape[0]

    @pl.when(pl.program_id(0) < n_tiles_ref[0])
    def _():
        x = x_ref[...]
        acc = None
        for c0 in range(0, d_expert, chunk):
            g = _dot(x, w_gu_ref[:, c0:c0 + chunk]) + b_gu_ref[:, c0:c0 + chunk]
            u = _dot(x, w_gu_ref[:, d_expert + c0:d_expert + c0 + chunk]) + b_gu_ref[:, d_expert + c0:d_expert + c0 + chunk]
            g = jnp.minimum(g, SWIGLU_LIMIT)
            u = jnp.clip(u, -SWIGLU_LIMIT, SWIGLU_LIMIT)
            a = ((u + 1.0) * g * _sigmoid(SWIGLU_ALPHA * g)).astype(BF16)
            part = _dot(a, w_down_ref[c0:c0 + chunk, :])
            acc = part if acc is None else acc + part
        o_ref[...] = acc + b_down_ref[...]


def _moe_experts(x_sorted, tile_expert, n_tiles, w_gu, b_gu, w_down, b_down):
    n_rows, d = x_sorted.shape
    n_exp, _, d_gu = w_gu.shape
    d_expert = w_down.shape[1]
    max_tiles = n_rows // MOE_TILE

    def row_index(i, te, nt):
        return (jnp.minimum(i, nt[0] - 1), 0)

    def expert_index(i, te, nt):
        return (te[i], 0, 0)

    grid_spec = pltpu.PrefetchScalarGridSpec(
        num_scalar_prefetch=2,
        grid=(max_tiles,),
        in_specs=[pl.BlockSpec((MOE_TILE, d), row_index),
                  pl.BlockSpec((None, d, d_gu), expert_index),
                  pl.BlockSpec((None, 1, d_gu), expert_index),
                  pl.BlockSpec((None, d_expert, d), expert_index),
                  pl.BlockSpec((None, 1, d), expert_index)],
        out_specs=pl.BlockSpec((MOE_TILE, d), row_index))
    return pl.pallas_call(
        functools.partial(_moe_body, chunk=min(512, d_expert)),
        grid_spec=grid_spec,
        out_shape=jax.ShapeDtypeStruct((n_rows, d), F32),
        compiler_params=_params(1),
        name="moe",
    )(tile_expert, n_tiles, x_sorted, w_gu, b_gu.reshape(n_exp, 1, d_gu), w_down, b_down.reshape(n_exp, 1, d))


def _route(logits, n_exp):
    n_tok = logits.shape[0]
    top_v, top_e = lax.top_k(logits, TOP_K)
    gates = jax.nn.softmax(top_v, axis=-1)
    flat_e = top_e.reshape(-1).astype(jnp.int32)
    order = jnp.argsort(flat_e)
    sorted_e = flat_e[order]
    counts = jnp.bincount(flat_e, length=n_exp).astype(jnp.int32)
    padded = (counts + MOE_TILE - 1) // MOE_TILE * MOE_TILE
    start = jnp.cumsum(counts) - counts
    pend = jnp.cumsum(padded)
    pstart = pend - padded
    dest = pstart[sorted_e] + jnp.arange(n_tok * TOP_K, dtype=jnp.int32) - start[sorted_e]
    max_tiles = (n_tok * TOP_K + MOE_TILE - 1) // MOE_TILE + n_exp
    n_rows = max_tiles * MOE_TILE
    row_tok = jnp.full((n_rows,), n_tok, jnp.int32).at[dest].set((order // TOP_K).astype(jnp.int32))
    pos = jnp.zeros((n_tok * TOP_K,), jnp.int32).at[order].set(dest).reshape(n_tok, TOP_K)
    n_tiles = (pend[-1] // MOE_TILE).astype(jnp.int32)
    tile_id = jnp.minimum(jnp.arange(max_tiles, dtype=jnp.int32), n_tiles - 1)
    tile_expert = jnp.minimum(jnp.searchsorted(pend, tile_id * MOE_TILE, side="right"), n_exp - 1).astype(jnp.int32)
    return gates, pos, row_tok, tile_expert, n_tiles.reshape(1)


def _final_body(x1_ref, y_ref, gt2_ref, g_ref, o_ref):
    o_ref[...] = x1_ref[...] + gt2_ref[...] * _rms(y_ref[...], g_ref[...])


def _final(x1, y, gt2, g, *, per_token, rows_per_seq):
    rows, d = x1.shape
    tm = min(TOKEN_TILE, rows)
    row_spec = pl.BlockSpec((tm, d), lambda i: (i, 0))
    return pl.pallas_call(
        _final_body,
        grid=(rows // tm,),
        in_specs=[row_spec, row_spec, _mod_spec(per_token, tm, d, rows_per_seq),
                  pl.BlockSpec((1, d), lambda i: (0, 0))],
        out_specs=row_spec,
        out_shape=jax.ShapeDtypeStruct((rows, d), F32),
        compiler_params=_params(1),
        name="final",
    )(x1, y, gt2, g.reshape(1, d))


def kernel(x_prompt, x_sample, cache_k, cache_v, state_conv, page_table, c_prompt, c_sample, w_mod, b_mod, g_pre_mix, g_post_mix, w_in, b_sb, w_att_out, w_dw, b_dw, ln_conv_g, ln_conv_b, w_conv_out, b_conv_out, w_out, g_pre_ffn, g_post_ffn, w_router, b_router, w_gu, b_gu, w_down, b_down):
    depth = w_mod.shape[0]
    assert depth == 1, "single-layer trunk"
    n, s, d = x_prompt.shape
    nb, t_new, _ = x_sample.shape
    d_att = w_att_out.shape[1]
    d_conv = w_dw.shape[2]
    width = w_dw.shape[1]
    n_exp = w_router.shape[2]
    head_dim = d_att // N_HEADS
    page = cache_k.shape[2]
    assert width - 1 <= HALO and s % TOKEN_TILE == 0
    l = 0

    row2 = lambda a: a.reshape(1, -1)
    w_in_b = w_in[l].astype(BF16)
    merge_w = (w_att_out[l].astype(BF16), row2(ln_conv_g[l]), row2(ln_conv_b[l]),
               w_conv_out[l].astype(BF16), row2(b_conv_out[l]), w_out[l].astype(BF16),
               row2(g_post_mix[l]), row2(g_pre_ffn[l]), w_router[l].astype(BF16), row2(b_router[l]))

    n_seq = n + nb
    c_all = jnp.pad(jnp.concatenate([c_prompt, c_sample], axis=0), ((0, -n_seq % 8), (0, 0)))
    mod = _modulation(c_all, w_mod[l], b_mod[l])
    mod_p = [m.reshape(n, 1, d) for m in jnp.split(mod[:n], 6, axis=-1)]
    mod_s = jnp.split(jnp.repeat(mod[n:n_seq], t_new, axis=0), 6, axis=-1)

    def mixing(x, mods, attend, conv, per_token, rows_per_seq):
        sh1, sc1, gt1, sh2, sc2, _ = mods
        q, k, v, kb, vb, u, ga, gc = _inproj(x, sc1, sh1, g_pre_mix[l], w_in_b, d_att=d_att, d_conv=d_conv,
                                             per_token=per_token, rows_per_seq=rows_per_seq)
        att = attend(q, k, v, kb, vb)
        dw = conv(u)
        x1, h2, logits = _merge(att, dw, ga, gc, x, gt1, sc2, sh2, merge_w,
                                per_token=per_token, rows_per_seq=rows_per_seq)
        return x1, h2, logits, k, v, u

    def attend_prompt(q, k, v, kb, vb):
        shape = (n, s, d_att)
        return _sb_attention_prompt(q.reshape(shape), kb.reshape(shape), vb.reshape(shape),
                                    b_sb[l].astype(F32)).reshape(n * s, d_att)

    def conv_prompt(u):
        tm = min(512, s)
        dw = _depthwise_conv(u.reshape(n * s // HALO, HALO, d_conv), u.reshape(n, s, d_conv), w_dw[l], b_dw[l],
                             bs=1, tm=tm, tiles_per_seq=s // tm)
        return dw.reshape(n * s, d_conv)

    x1_p, h2_p, logits_p, k_p, v_p, u_p = mixing(x_prompt.reshape(n * s, d), mod_p, attend_prompt, conv_prompt,
                                                  False, s)

    pool_k = cache_k[l].reshape(-1, page, d_att)
    pool_v = cache_v[l].reshape(-1, page, d_att)

    def attend_sample(q, k, v, kb, vb):
        shape = (nb, t_new, d_att)
        return _sb_attention_sample(q.reshape(shape), k.reshape(shape), v.reshape(shape), pool_k, pool_v,
                                    page_table, b_sb[l]).reshape(nb * t_new, d_att)

    def conv_sample(u):
        t_pad = -t_new % 8
        cur = jnp.pad(u.reshape(nb, t_new, d_conv), ((0, 0), (0, t_pad), (0, 0)))
        prev = jnp.pad(state_conv[l], ((0, 0), (HALO - (width - 1), 0), (0, 0)))
        dw = _depthwise_conv(prev, cur, w_dw[l], b_dw[l], bs=min(16, nb), tm=t_new + t_pad, tiles_per_seq=None)
        return dw[:, :t_new].reshape(nb * t_new, d_conv)

    x1_s, h2_s, logits_s, k_s, v_s, u_s = mixing(x_sample.reshape(nb * t_new, d), mod_s, attend_sample, conv_sample,
                                                  True, t_new)

    n_tok = n * s + nb * t_new
    h2 = jnp.concatenate([h2_p, h2_s], axis=0)
    gates, pos, row_tok, tile_expert, n_tiles = _route(jnp.concatenate([logits_p, logits_s], axis=0), n_exp)
    x_sorted = jnp.concatenate([h2, jnp.zeros((1, d), h2.dtype)], axis=0)[row_tok]
    y_sorted = _moe_experts(x_sorted, tile_expert, n_tiles, w_gu[l].astype(BF16), b_gu[l],
                            w_down[l].astype(BF16), b_down[l])
    y = jnp.sum(y_sorted[pos] * gates[:, :, None], axis=1)

    y_prompt = _final(x1_p, y[:n * s], mod_p[5], g_post_ffn[l], per_token=False, rows_per_seq=s)
    y_sample = _final(x1_s, y[n * s:], mod_s[5], g_post_ffn[l], per_token=True, rows_per_seq=t_new)

    kv_p = (1, n, s, N_HEADS, head_dim)
    kv_s = (1, nb, t_new, N_HEADS, head_dim)
    keep = width - 1
    conv_p = u_p.reshape(n, s, d_conv)[:, s - keep:]
    conv_s = jnp.concatenate([state_conv[l], u_s.reshape(nb, t_new, d_conv)], axis=1)[:, -keep:]
    return (y_prompt.reshape(n, s, d), y_sample.reshape(nb, t_new, d),
            k_p.reshape(kv_p), v_p.reshape(kv_p), conv_p[None],
            k_s.reshape(kv_s), v_s.reshape(kv_s), conv_s[None])
```

```python
import functools

import jax
import jax.numpy as jnp
from jax import lax
from jax.experimental import pallas as pl
from jax.experimental.pallas import tpu as pltpu

F32 = jnp.float32
BF16 = jnp.bfloat16

N_HEADS = 8
TOP_K = 4
SWIGLU_LIMIT = 7.0
SWIGLU_ALPHA = 1.702
RMS_EPS = 1e-6
LN_EPS = 1e-5

VMEM_LIMIT_BYTES = 48 * 1024 * 1024
HALO = 32
TOKEN_TILE = 256
ATT_TILE = 256
MOE_TILE = 512
PAGES_PER_STEP = 8


def _params(n_axes):
    return pltpu.CompilerParams(dimension_semantics=("arbitrary",) * n_axes,
                                vmem_limit_bytes=VMEM_LIMIT_BYTES)


def _dot(a, b):
    return jnp.dot(a, b, preferred_element_type=F32)


def _dot_nt(a, b):
    return lax.dot_general(a, b, (((1,), (1,)), ((), ())), preferred_element_type=F32)


def _rms(x, g):
    return x * lax.rsqrt(jnp.mean(x * x, axis=-1, keepdims=True) + RMS_EPS) * g


def _sigmoid(x):
    return 1.0 / (1.0 + jnp.exp(-x))


def _mod_body(c_ref, w_ref, b_ref, o_ref):
    c = c_ref[...]
    s = (c * _sigmoid(c)).astype(BF16)
    o_ref[...] = _dot(s, w_ref[...].astype(BF16)) + b_ref[...]


def _modulation(c, w_mod, b_mod):
    n, d = c.shape
    n_out = w_mod.shape[1]
    tn = n_out // 6
    return pl.pallas_call(
        _mod_body,
        grid=(n_out // tn,),
        in_specs=[pl.BlockSpec((n, d), lambda j: (0, 0)),
                  pl.BlockSpec((d, tn), lambda j: (0, j)),
                  pl.BlockSpec((1, tn), lambda j: (0, j))],
        out_specs=pl.BlockSpec((n, tn), lambda j: (0, j)),
        out_shape=jax.ShapeDtypeStruct((n, n_out), F32),
        compiler_params=_params(1),
        name="mod",
    )(c, w_mod, b_mod.reshape(1, n_out))


def _inproj_body(x_ref, sc_ref, sh_ref, g_ref, w_ref,
                 q_ref, k_ref, v_ref, kb_ref, vb_ref, u_ref, ga_ref, gc_ref, *, d_att, d_conv, q_scale):
    d_model = x_ref.shape[-1]
    h = _rms(x_ref[...], g_ref[...]) * (1.0 + sc_ref[...]) + sh_ref[...]
    hb = h.astype(BF16)

    def proj(lo, width):
        return _dot(hb, w_ref[:, lo:lo + width])

    q_ref[...] = (proj(0, d_att) * q_scale).astype(BF16)
    k = proj(d_att, d_att)
    k_ref[...] = k
    kb_ref[...] = k.astype(BF16)
    v = proj(2 * d_att, d_att)
    v_ref[...] = v
    vb_ref[...] = v.astype(BF16)
    ca = proj(3 * d_att, d_conv)
    cb = proj(3 * d_att + d_conv, d_conv)
    u_ref[...] = ca * _sigmoid(cb)
    ga_ref[...] = _sigmoid(proj(3 * d_att + 2 * d_conv, d_model))
    gc_ref[...] = _sigmoid(proj(3 * d_att + 2 * d_conv + d_model, d_model))


def _mod_spec(per_token, tm, d, rows_per_seq):
    if per_token:
        return pl.BlockSpec((tm, d), lambda i: (i, 0))
    tiles_per_seq = rows_per_seq // tm
    return pl.BlockSpec((None, 1, d), lambda i: (i // tiles_per_seq, 0, 0))


def _inproj(x, sc, sh, g, w_in, *, d_att, d_conv, per_token, rows_per_seq):
    rows, d = x.shape
    tm = min(TOKEN_TILE, rows)
    head_dim = d_att // N_HEADS
    row_spec = lambda width: pl.BlockSpec((tm, width), lambda i: (i, 0))
    mod_spec = _mod_spec(per_token, tm, d, rows_per_seq)
    widths = (d_att, d_att, d_att, d_att, d_att, d_conv, d, d)
    dtypes = (BF16, F32, F32, BF16, BF16, F32, F32, F32)
    return pl.pallas_call(
        functools.partial(_inproj_body, d_att=d_att, d_conv=d_conv, q_scale=head_dim ** -0.5),
        grid=(rows // tm,),
        in_specs=[row_spec(d), mod_spec, mod_spec,
                  pl.BlockSpec((1, d), lambda i: (0, 0)),
                  pl.BlockSpec(w_in.shape, lambda i: (0, 0))],
        out_specs=[row_spec(w) for w in widths],
        out_shape=[jax.ShapeDtypeStruct((rows, w), t) for w, t in zip(widths, dtypes)],
        compiler_params=_params(1),
        name="inproj",
    )(x, sc, sh, g.reshape(1, d), w_in)


def _suffix_matrix(tile):
    row = lax.broadcasted_iota(jnp.int32, (tile, tile), 0)
    col = lax.broadcasted_iota(jnp.int32, (tile, tile), 1)
    tri = (row >= col).astype(BF16)
    return jnp.concatenate([tri, tri], axis=0)


def _sb_suffix(z, tri2, valid):
    neg_abs = pltpu.bitcast(pltpu.bitcast(z, jnp.uint32) | jnp.uint32(0x80000000), F32)
    sp = jnp.maximum(z, 0.0) + jnp.log(1.0 + jnp.exp(neg_abs))
    if valid is not None:
        sp = jnp.where(valid, sp, 0.0)
    hi = sp.astype(BF16)
    lo = (sp - hi.astype(F32)).astype(BF16)
    return _dot(jnp.concatenate([hi, lo], axis=1), tri2)


def _sb_weights(z, right, valid):
    p = jnp.exp(z - right)
    if valid is not None:
        p = jnp.where(valid, p, 0.0)
    return p.astype(BF16)


def _sb_prompt_body(bias_ref, q_ref, k_ref, v_ref, o_ref,
                    qa_ref, z_ref, p_ref, acc_ref, ucur_ref, uprev_ref, *, tile, head_dim):
    hp = pl.program_id(1)
    qi = pl.program_id(2)
    pair = 2 * head_dim
    lane = lax.broadcasted_iota(jnp.int32, (1, pair), 1)
    row = lax.broadcasted_iota(jnp.int32, (2 * tile, tile), 0)
    col = lax.broadcasted_iota(jnp.int32, (2 * tile, tile), 1)
    causal = col < jnp.where(row >= tile, row - tile, row)
    tri2 = _suffix_matrix(tile)
    ones = jnp.ones((tile, pair), BF16)

    q = q_ref[0]
    for h in range(2):
        qa_ref[h * tile:(h + 1) * tile, 0:pair] = jnp.where(
            (lane >= h * head_dim) & (lane < (h + 1) * head_dim), q, jnp.zeros_like(q))
        b = bias_ref[2 * hp + h]
        b1 = b.astype(BF16).astype(F32)
        b2 = (b - b1).astype(BF16).astype(F32)
        b3 = (b - b1 - b2).astype(BF16).astype(F32)
        offset = jnp.where(lane == 0, b1, jnp.where(lane == 1, b2, jnp.where(lane == 2, b3, 0.0)))
        qa_ref[h * tile:(h + 1) * tile, pair:2 * pair] = jnp.broadcast_to(offset, (tile, pair)).astype(BF16)

    def logits(t, slot):
        off = pl.multiple_of(jnp.maximum(qi - t, 0) * tile, tile)
        ka = jnp.concatenate([k_ref[0, pl.ds(off, tile), :], ones], axis=1)
        z_ref[slot] = _dot_nt(qa_ref[...], ka)

    def weights(slot, valid):
        right = _sb_suffix(z_ref[slot], tri2, valid)
        p_ref[slot] = _sb_weights(z_ref[slot], right, valid)
        return right[:, 0:1]

    def values(t, slot):
        off = pl.multiple_of((qi - t) * tile, tile)
        acc_ref[...] += jnp.exp(-uprev_ref[...]) * _dot(p_ref[slot], v_ref[0, pl.ds(off, tile), :])

    logits(0, 0)
    logits(1, 1)
    ucur_ref[...] = weights(0, causal)
    uprev_ref[...] = jnp.zeros_like(uprev_ref)
    acc_ref[...] = jnp.zeros_like(acc_ref)

    def stage(t, slot):
        values(t - 1, 1 - slot)
        total = weights(slot, None)
        logits(t + 1, 1 - slot)
        ucur = ucur_ref[...]
        uprev_ref[...] = ucur
        ucur_ref[...] = ucur + total

    def two_stages(u, carry):
        stage(1 + 2 * u, 1)
        stage(2 + 2 * u, 0)
        return carry

    lax.fori_loop(0, qi // 2, two_stages, 0)

    @pl.when(qi % 2 == 1)
    def _():
        stage(qi, 1)
        values(qi, 1)

    @pl.when(qi % 2 == 0)
    def _():
        values(qi, 0)

    o_ref[0] = jnp.where(lane < head_dim, acc_ref[0:tile], acc_ref[tile:2 * tile]).astype(o_ref.dtype)


def _sb_attention_prompt(q, k, v, bias):
    n, s, d_att = q.shape
    head_dim = d_att // N_HEADS
    tile = min(ATT_TILE, s)
    pair = 2 * head_dim
    grid_spec = pltpu.PrefetchScalarGridSpec(
        num_scalar_prefetch=1,
        grid=(n, N_HEADS // 2, s // tile),
        in_specs=[pl.BlockSpec((1, tile, pair), lambda b, hp, qi, bias: (b, qi, hp)),
                  pl.BlockSpec((1, s, pair), lambda b, hp, qi, bias: (b, 0, hp)),
                  pl.BlockSpec((1, s, pair), lambda b, hp, qi, bias: (b, 0, hp))],
        out_specs=pl.BlockSpec((1, tile, pair), lambda b, hp, qi, bias: (b, qi, hp)),
        scratch_shapes=[pltpu.VMEM((2 * tile, 2 * pair), BF16),
                        pltpu.VMEM((2, 2 * tile, tile), F32),
                        pltpu.VMEM((2, 2 * tile, tile), BF16),
                        pltpu.VMEM((2 * tile, pair), F32),
                        pltpu.VMEM((2 * tile, 1), F32),
                        pltpu.VMEM((2 * tile, 1), F32)])
    return pl.pallas_call(
        functools.partial(_sb_prompt_body, tile=tile, head_dim=head_dim),
        grid_spec=grid_spec,
        out_shape=jax.ShapeDtypeStruct((n, s, d_att), BF16),
        compiler_params=_params(3),
        name="sb_prompt",
    )(bias, q, k, v)


def _sb_sample_body(pt_ref, qbd_ref, bias_ref, knew_ref, vnew_ref, *rest, page, t_new, n_steps):
    pages = rest[:2 * PAGES_PER_STEP]
    o_ref, kt_ref, vt_ref, acc_ref, used_ref = rest[2 * PAGES_PER_STEP:]
    g = pl.program_id(1)
    rows, d_att = acc_ref.shape
    head_dim = d_att // N_HEADS
    tri2 = _suffix_matrix(page)
    qbd = qbd_ref[0]
    bias = bias_ref[...]

    @pl.when(g == 0)
    def _():
        qidx = lax.broadcasted_iota(jnp.int32, (rows, page), 0) // N_HEADS
        kidx = lax.broadcasted_iota(jnp.int32, (rows, page), 1)
        valid = kidx < qidx
        z = _dot(qbd, knew_ref[0].astype(BF16)) + bias
        right = _sb_suffix(z, tri2, valid)
        acc_ref[...] = _dot_nt(_sb_weights(z, right, valid), vnew_ref[0].astype(BF16))
        used_ref[...] = right[:, 0:1]

    for i in range(PAGES_PER_STEP):
        kt_ref[:, i * page:(i + 1) * page] = pages[2 * i][0].astype(BF16)
        vt_ref[:, i * page:(i + 1) * page] = pages[2 * i + 1][0].astype(BF16)
    z_wide = _dot(qbd, kt_ref[...])
    z = jnp.concatenate([z_wide[:, i * page:(i + 1) * page] + bias for i in range(PAGES_PER_STEP)], axis=0)
    right = _sb_suffix(z, tri2, None)
    used = used_ref[...]
    used_rows = []
    for i in range(PAGES_PER_STEP):
        used_rows.append(used)
        used = used + right[i * rows:(i + 1) * rows, 0:1]
    used_ref[...] = used
    p = _sb_weights(z, right + jnp.concatenate(used_rows, axis=0), None)
    p_wide = jnp.concatenate([p[i * rows:(i + 1) * rows] for i in range(PAGES_PER_STEP)], axis=1)
    acc_ref[...] += _dot_nt(p_wide, vt_ref[...])

    @pl.when(g == n_steps - 1)
    def _():
        head = lax.broadcasted_iota(jnp.int32, (rows, d_att), 0) % N_HEADS
        lane_head = lax.broadcasted_iota(jnp.int32, (rows, d_att), 1) // head_dim
        own = jnp.where(head == lane_head, acc_ref[...], 0.0)
        o_ref[0] = jnp.sum(own.reshape(t_new, N_HEADS, d_att), axis=1)


def _sb_attention_sample(q, k_new, v_new, cache_k, cache_v, page_table, bias):
    nb, t_new, d_att = q.shape
    head_dim = d_att // N_HEADS
    page = cache_k.shape[2]
    n_pages = page_table.shape[1]
    n_steps = n_pages // PAGES_PER_STEP
    rows = t_new * N_HEADS
    head_mask = (jnp.arange(d_att)[None, :] // head_dim == jnp.arange(N_HEADS)[:, None]).astype(q.dtype)
    qbd = (q[:, :, None, :] * head_mask[None, None]).reshape(nb, rows, d_att)
    bias_rows = jnp.broadcast_to(jnp.tile(bias.astype(F32), t_new)[:, None], (rows, page))
    new_keys = lambda a: jnp.pad(a.transpose(0, 2, 1), ((0, 0), (0, 0), (0, page - t_new)))

    def page_spec(i):
        def index(b, g, pt):
            return (pt[b, n_pages - 1 - (g * PAGES_PER_STEP + i)], 0, 0)
        return pl.BlockSpec((1, d_att, page), index)

    page_specs, page_args = [], []
    for i in range(PAGES_PER_STEP):
        page_specs += [page_spec(i), page_spec(i)]
        page_args += [cache_k, cache_v]
    seq_spec = lambda r, c: pl.BlockSpec((1, r, c), lambda b, g, pt: (b, 0, 0))
    grid_spec = pltpu.PrefetchScalarGridSpec(
        num_scalar_prefetch=1,
        grid=(nb, n_steps),
        in_specs=[seq_spec(rows, d_att), pl.BlockSpec((rows, page), lambda b, g, pt: (0, 0)),
                  seq_spec(d_att, page), seq_spec(d_att, page)] + page_specs,
        out_specs=seq_spec(t_new, d_att),
        scratch_shapes=[pltpu.VMEM((d_att, PAGES_PER_STEP * page), BF16),
                        pltpu.VMEM((d_att, PAGES_PER_STEP * page), BF16),
                        pltpu.VMEM((rows, d_att), F32), pltpu.VMEM((rows, 1), F32)])
    return pl.pallas_call(
        functools.partial(_sb_sample_body, page=page, t_new=t_new, n_steps=n_steps),
        grid_spec=grid_spec,
        out_shape=jax.ShapeDtypeStruct((nb, t_new, d_att), F32),
        compiler_params=_params(2),
        name="sb_sample",
    )(page_table, qbd, bias_rows, new_keys(k_new), new_keys(v_new), *page_args)


def _conv_body(prev_ref, cur_ref, w_ref, b_ref, o_ref, ext_ref, *, width, tiles_per_seq, chunk):
    bs, tm, _ = cur_ref.shape
    prev = prev_ref[...]
    if tiles_per_seq is not None:
        prev = jnp.where(pl.program_id(0) % tiles_per_seq == 0, 0.0, prev)
    ext_ref[:, 0:HALO, :] = prev
    ext_ref[:, HALO:, :] = cur_ref[...]
    first = HALO - (width - 1)
    for b in range(bs):
        for r0 in range(0, tm, chunk):
            acc = ext_ref[b, first + r0:first + r0 + chunk, :] * w_ref[0:1, :]
            for w in range(1, width):
                acc += ext_ref[b, first + r0 + w:first + r0 + w + chunk, :] * w_ref[w:w + 1, :]
            o_ref[b, r0:r0 + chunk, :] = acc + b_ref[...]


def _depthwise_conv(prev, cur, w_dw, b_dw, *, bs, tm, tiles_per_seq):
    n, rows, c = cur.shape
    width = w_dw.shape[0]
    w_pad = jnp.pad(w_dw, ((0, HALO - width), (0, 0)))
    if tiles_per_seq is None:
        grid = (n // bs,)
        prev_index = lambda i: (i, 0, 0)
        cur_index = lambda i: (i, 0, 0)
    else:
        grid = (n * tiles_per_seq,)
        halo_per_tile = tm // HALO
        prev_index = lambda i: (jnp.maximum(i * halo_per_tile - 1, 0), 0, 0)
        cur_index = lambda i: (i // tiles_per_seq, i % tiles_per_seq, 0)
    return pl.pallas_call(
        functools.partial(_conv_body, width=width, tiles_per_seq=tiles_per_seq, chunk=min(64, tm)),
        grid=grid,
        in_specs=[pl.BlockSpec((bs, HALO, c), prev_index),
                  pl.BlockSpec((bs, tm, c), cur_index),
                  pl.BlockSpec((HALO, c), lambda i: (0, 0)),
                  pl.BlockSpec((1, c), lambda i: (0, 0))],
        out_specs=pl.BlockSpec((bs, tm, c), cur_index),
        out_shape=jax.ShapeDtypeStruct((n, rows, c), F32),
        scratch_shapes=[pltpu.VMEM((bs, HALO + tm, c), F32)],
        compiler_params=_params(1),
        name="conv",
    )(prev, cur, w_pad, b_dw.reshape(1, c))


def _merge_body(att_ref, dw_ref, ga_ref, gc_ref, x_ref, gt1_ref, sc2_ref, sh2_ref,
                w_att_ref, ln_g_ref, ln_b_ref, w_conv_ref, b_conv_ref, w_out_ref,
                g_post_ref, g_pre_ref, w_router_ref, b_router_ref,
                x1_ref, h2_ref, logit_ref):
    y_att = _dot(att_ref[...].astype(BF16), w_att_ref[...])
    dw = dw_ref[...]
    mu = jnp.mean(dw, axis=-1, keepdims=True)
    xc = dw - mu
    ln = xc * lax.rsqrt(jnp.mean(xc * xc, axis=-1, keepdims=True) + LN_EPS) * ln_g_ref[...] + ln_b_ref[...]
    act = ln * _sigmoid(ln)
    y_conv = _dot(act.astype(BF16), w_conv_ref[...]) + b_conv_ref[...]
    mixed = ga_ref[...] * y_att + gc_ref[...] * y_conv
    out = _dot(mixed.astype(BF16), w_out_ref[...])
    x1 = x_ref[...] + gt1_ref[...] * _rms(out, g_post_ref[...])
    x1_ref[...] = x1
    h2 = (_rms(x1, g_pre_ref[...]) * (1.0 + sc2_ref[...]) + sh2_ref[...]).astype(BF16)
    h2_ref[...] = h2
    logit_ref[...] = _dot(h2, w_router_ref[...]) + b_router_ref[...]


def _merge(att, dw, ga, gc, x, gt1, sc2, sh2, weights, *, per_token, rows_per_seq):
    rows, d = x.shape
    tm = min(TOKEN_TILE, rows)
    n_exp = weights[-2].shape[1]
    row_spec = lambda a: pl.BlockSpec((tm, a.shape[1]), lambda i: (i, 0))
    mod_spec = _mod_spec(per_token, tm, d, rows_per_seq)
    full_spec = lambda a: pl.BlockSpec(a.shape, lambda i: (0, 0))
    out_widths = (d, d, n_exp)
    out_dtypes = (F32, BF16, F32)
    return pl.pallas_call(
        _merge_body,
        grid=(rows // tm,),
        in_specs=[row_spec(a) for a in (att, dw, ga, gc, x)] + [mod_spec] * 3 + [full_spec(w) for w in weights],
        out_specs=[pl.BlockSpec((tm, w), lambda i: (i, 0)) for w in out_widths],
        out_shape=[jax.ShapeDtypeStruct((rows, w), t) for w, t in zip(out_widths, out_dtypes)],
        compiler_params=_params(1),
        name="merge",
    )(att, dw, ga, gc, x, gt1, sc2, sh2, *weights)


def _moe_body(tile_expert_ref, n_tiles_ref, x_ref, w_gu_ref, b_gu_ref, w_down_ref, b_down_ref, o_ref, *, chunk):
    d_expert = w_down_ref.shape[0]

    @pl.when(pl.program_id(0) < n_tiles_ref[0])
    def _():
        x = x_ref[...]
        acc = None
        for c0 in range(0, d_expert, chunk):
            g = _dot(x, w_gu_ref[:, c0:c0 + chunk]) + b_gu_ref[:, c0:c0 + chunk]
            u = _dot(x, w_gu_ref[:, d_expert + c0:d_expert + c0 + chunk]) + b_gu_ref[:, d_expert + c0:d_expert + c0 + chunk]
            g = jnp.minimum(g, SWIGLU_LIMIT)
            u = jnp.clip(u, -SWIGLU_LIMIT, SWIGLU_LIMIT)
            a = ((u + 1.0) * g * _sigmoid(SWIGLU_ALPHA * g)).astype(BF16)
            part = _dot(a, w_down_ref[c0:c0 + chunk, :])
            acc = part if acc is None else acc + part
        o_ref[...] = acc + b_down_ref[...]


def _moe_experts(x_sorted, tile_expert, n_tiles, w_gu, b_gu, w_down, b_down):
    n_rows, d = x_sorted.shape
    n_exp, _, d_gu = w_gu.shape
    d_expert = w_down.shape[1]
    max_tiles = n_rows // MOE_TILE

    def row_index(i, te, nt):
        return (jnp.minimum(i, nt[0] - 1), 0)

    def expert_index(i, te, nt):
        return (te[i], 0, 0)

    grid_spec = pltpu.PrefetchScalarGridSpec(
        num_scalar_prefetch=2,
        grid=(max_tiles,),
        in_specs=[pl.BlockSpec((MOE_TILE, d), row_index),
                  pl.BlockSpec((None, d, d_gu), expert_index),
                  pl.BlockSpec((None, 1, d_gu), expert_index),
                  pl.BlockSpec((None, d_expert, d), expert_index),
                  pl.BlockSpec((None, 1, d), expert_index)],
        out_specs=pl.BlockSpec((MOE_TILE, d), row_index))
    return pl.pallas_call(
        functools.partial(_moe_body, chunk=min(512, d_expert)),
        grid_spec=grid_spec,
        out_shape=jax.ShapeDtypeStruct((n_rows, d), F32),
        compiler_params=_params(1),
        name="moe",
    )(tile_expert, n_tiles, x_sorted, w_gu, b_gu.reshape(n_exp, 1, d_gu), w_down, b_down.reshape(n_exp, 1, d))


def _route(logits, n_exp):
    n_tok = logits.shape[0]
    top_v, top_e = lax.top_k(logits, TOP_K)
    gates = jax.nn.softmax(top_v, axis=-1)
    flat_e = top_e.reshape(-1).astype(jnp.int32)
    order = jnp.argsort(flat_e)
    sorted_e = flat_e[order]
    counts = jnp.bincount(flat_e, length=n_exp).astype(jnp.int32)
    padded = (counts + MOE_TILE - 1) // MOE_TILE * MOE_TILE
    start = jnp.cumsum(counts) - counts
    pend = jnp.cumsum(padded)
    pstart = pend - padded
    dest = pstart[sorted_e] + jnp.arange(n_tok * TOP_K, dtype=jnp.int32) - start[sorted_e]
    max_tiles = (n_tok * TOP_K + MOE_TILE - 1) // MOE_TILE + n_exp
    n_rows = max_tiles * MOE_TILE
    row_tok = jnp.full((n_rows,), n_tok, jnp.int32).at[dest].set((order // TOP_K).astype(jnp.int32))
    pos = jnp.zeros((n_tok * TOP_K,), jnp.int32).at[order].set(dest).reshape(n_tok, TOP_K)
    n_tiles = (pend[-1] // MOE_TILE).astype(jnp.int32)
    tile_id = jnp.minimum(jnp.arange(max_tiles, dtype=jnp.int32), n_tiles - 1)
    tile_expert = jnp.minimum(jnp.searchsorted(pend, tile_id * MOE_TILE, side="right"), n_exp - 1).astype(jnp.int32)
    return gates, pos, row_tok, tile_expert, n_tiles.reshape(1)


def _final_body(x1_ref, y_ref, gt2_ref, g_ref, o_ref):
    o_ref[...] = x1_ref[...] + gt2_ref[...] * _rms(y_ref[...], g_ref[...])


def _final(x1, y, gt2, g, *, per_token, rows_per_seq):
    rows, d = x1.shape
    tm = min(TOKEN_TILE, rows)
    row_spec = pl.BlockSpec((tm, d), lambda i: (i, 0))
    return pl.pallas_call(
        _final_body,
        grid=(rows // tm,),
        in_specs=[row_spec, row_spec, _mod_spec(per_token, tm, d, rows_per_seq),
                  pl.BlockSpec((1, d), lambda i: (0, 0))],
        out_specs=row_spec,
        out_shape=jax.ShapeDtypeStruct((rows, d), F32),
        compiler_params=_params(1),
        name="final",
    )(x1, y, gt2, g.reshape(1, d))


def kernel(x_prompt, x_sample, cache_k, cache_v, state_conv, page_table, c_prompt, c_sample, w_mod, b_mod, g_pre_mix, g_post_mix, w_in, b_sb, w_att_out, w_dw, b_dw, ln_conv_g, ln_conv_b, w_conv_out, b_conv_out, w_out, g_pre_ffn, g_post_ffn, w_router, b_router, w_gu, b_gu, w_down, b_down):
    depth = w_mod.shape[0]
    assert depth == 1, "single-layer trunk"
    n, s, d = x_prompt.shape
    nb, t_new, _ = x_sample.shape
    d_att = w_att_out.shape[1]
    d_conv = w_dw.shape[2]
    width = w_dw.shape[1]
    n_exp = w_router.shape[2]
    head_dim = d_att // N_HEADS
    page = cache_k.shape[2]
    assert width - 1 <= HALO and s % TOKEN_TILE == 0
    l = 0

    row2 = lambda a: a.reshape(1, -1)
    w_in_b = w_in[l].astype(BF16)
    merge_w = (w_att_out[l].astype(BF16), row2(ln_conv_g[l]), row2(ln_conv_b[l]),
               w_conv_out[l].astype(BF16), row2(b_conv_out[l]), w_out[l].astype(BF16),
               row2(g_post_mix[l]), row2(g_pre_ffn[l]), w_router[l].astype(BF16), row2(b_router[l]))

    n_seq = n + nb
    c_all = jnp.pad(jnp.concatenate([c_prompt, c_sample], axis=0), ((0, -n_seq % 8), (0, 0)))
    mod = _modulation(c_all, w_mod[l], b_mod[l])
    mod_p = [m.reshape(n, 1, d) for m in jnp.split(mod[:n], 6, axis=-1)]
    mod_s = jnp.split(jnp.repeat(mod[n:n_seq], t_new, axis=0), 6, axis=-1)

    def mixing(x, mods, attend, conv, per_token, rows_per_seq):
        sh1, sc1, gt1, sh2, sc2, _ = mods
        q, k, v, kb, vb, u, ga, gc = _inproj(x, sc1, sh1, g_pre_mix[l], w_in_b, d_att=d_att, d_conv=d_conv,
                                             per_token=per_token, rows_per_seq=rows_per_seq)
        att = attend(q, k, v, kb, vb)
        dw = conv(u)
        x1, h2, logits = _merge(att, dw, ga, gc, x, gt1, sc2, sh2, merge_w,
                                per_token=per_token, rows_per_seq=rows_per_seq)
        return x1, h2, logits, k, v, u

    def attend_prompt(q, k, v, kb, vb):
        shape = (n, s, d_att)
        return _sb_attention_prompt(q.reshape(shape), kb.reshape(shape), vb.reshape(shape),
                                    b_sb[l].astype(F32)).reshape(n * s, d_att)

    def conv_prompt(u):
        tm = min(512, s)
        dw = _depthwise_conv(u.reshape(n * s // HALO, HALO, d_conv), u.reshape(n, s, d_conv), w_dw[l], b_dw[l],
                             bs=1, tm=tm, tiles_per_seq=s // tm)
        return dw.reshape(n * s, d_conv)

    x1_p, h2_p, logits_p, k_p, v_p, u_p = mixing(x_prompt.reshape(n * s, d), mod_p, attend_prompt, conv_prompt,
                                                  False, s)

    pool_k = cache_k[l].transpose(0, 2, 3, 1).reshape(-1, d_att, page)
    pool_v = cache_v[l].transpose(0, 2, 3, 1).reshape(-1, d_att, page)

    def attend_sample(q, k, v, kb, vb):
        shape = (nb, t_new, d_att)
        return _sb_attention_sample(q.reshape(shape), k.reshape(shape), v.reshape(shape), pool_k, pool_v,
                                    page_table, b_sb[l]).reshape(nb * t_new, d_att)

    def conv_sample(u):
        t_pad = -t_new % 8
        cur = jnp.pad(u.reshape(nb, t_new, d_conv), ((0, 0), (0, t_pad), (0, 0)))
        prev = jnp.pad(state_conv[l], ((0, 0), (HALO - (width - 1), 0), (0, 0)))
        dw = _depthwise_conv(prev, cur, w_dw[l], b_dw[l], bs=min(16, nb), tm=t_new + t_pad, tiles_per_seq=None)
        return dw[:, :t_new].reshape(nb * t_new, d_conv)

    x1_s, h2_s, logits_s, k_s, v_s, u_s = mixing(x_sample.reshape(nb * t_new, d), mod_s, attend_sample, conv_sample,
                                                  True, t_new)

    n_tok = n * s + nb * t_new
    h2 = jnp.concatenate([h2_p, h2_s], axis=0)
    gates, pos, row_tok, tile_expert, n_tiles = _route(jnp.concatenate([logits_p, logits_s], axis=0), n_exp)
    x_sorted = jnp.concatenate([h2, jnp.zeros((1, d), h2.dtype)], axis=0)[row_tok]
    y_sorted = _moe_experts(x_sorted, tile_expert, n_tiles, w_gu[l].astype(BF16), b_gu[l],
                            w_down[l].astype(BF16), b_down[l])
    y = jnp.sum(y_sorted[pos] * gates[:, :, None], axis=1)

    y_prompt = _final(x1_p, y[:n * s], mod_p[5], g_post_ffn[l], per_token=False, rows_per_seq=s)
    y_sample = _final(x1_s, y[n * s:], mod_s[5], g_post_ffn[l], per_token=True, rows_per_seq=t_new)

    kv_p = (1, n, s, N_HEADS, head_dim)
    kv_s = (1, nb, t_new, N_HEADS, head_dim)
    keep = width - 1
    conv_p = u_p.reshape(n, s, d_conv)[:, s - keep:]
    conv_s = jnp.concatenate([state_conv[l], u_s.reshape(nb, t_new, d_conv)], axis=1)[:, -keep:]
    return (y_prompt.reshape(n, s, d), y_sample.reshape(nb, t_new, d),
            k_p.reshape(kv_p), v_p.reshape(kv_p), conv_p[None],
            k_s.reshape(kv_s), v_s.reshape(kv_s), conv_s[None])
```

```python
import functools

import jax
import jax.numpy as jnp
from jax import lax
from jax.experimental import pallas as pl
from jax.experimental.pallas import tpu as pltpu

F32 = jnp.float32
BF16 = jnp.bfloat16

N_HEADS = 8
TOP_K = 4
SWIGLU_LIMIT = 7.0
SWIGLU_ALPHA = 1.702
RMS_EPS = 1e-6
LN_EPS = 1e-5

VMEM_LIMIT_BYTES = 48 * 1024 * 1024
HALO = 32
TOKEN_TILE = 256
ATT_TILE = 256
MOE_TILE = 512
LANES = 128
PAGES_PER_STEP = 8


def _params(n_axes):
    return pltpu.CompilerParams(dimension_semantics=("arbitrary",) * n_axes,
                                vmem_limit_bytes=VMEM_LIMIT_BYTES)


def _dot(a, b):
    return jnp.dot(a, b, preferred_element_type=F32)


def _dot_nt(a, b):
    return lax.dot_general(a, b, (((1,), (1,)), ((), ())), preferred_element_type=F32)


def _rms(x, g):
    return x * lax.rsqrt(jnp.mean(x * x, axis=-1, keepdims=True) + RMS_EPS) * g


def _sigmoid(x):
    return 1.0 / (1.0 + jnp.exp(-x))


def _mod_body(c_ref, w_ref, b_ref, o_ref):
    c = c_ref[...]
    s = (c * _sigmoid(c)).astype(BF16)
    o_ref[...] = _dot(s, w_ref[...].astype(BF16)) + b_ref[...]


def _modulation(c, w_mod, b_mod):
    n, d = c.shape
    n_out = w_mod.shape[1]
    tn = n_out // 6
    return pl.pallas_call(
        _mod_body,
        grid=(n_out // tn,),
        in_specs=[pl.BlockSpec((n, d), lambda j: (0, 0)),
                  pl.BlockSpec((d, tn), lambda j: (0, j)),
                  pl.BlockSpec((1, tn), lambda j: (0, j))],
        out_specs=pl.BlockSpec((n, tn), lambda j: (0, j)),
        out_shape=jax.ShapeDtypeStruct((n, n_out), F32),
        compiler_params=_params(1),
        name="mod",
    )(c, w_mod, b_mod.reshape(1, n_out))


def _inproj_body(x_ref, sc_ref, sh_ref, g_ref, w_ref, w_kvt_ref,
                 q_ref, kt_ref, vt_ref, kb_ref, vb_ref, u_ref, ga_ref, gc_ref, *, d_att, d_conv, q_scale):
    d_model = x_ref.shape[-1]
    h = _rms(x_ref[...], g_ref[...]) * (1.0 + sc_ref[...]) + sh_ref[...]
    hb = h.astype(BF16)

    def proj(lo, width):
        return _dot(hb, w_ref[:, lo:lo + width])

    q_ref[...] = (proj(0, d_att) * q_scale).astype(BF16)
    kb_ref[...] = proj(d_att, d_att).astype(BF16)
    vb_ref[...] = proj(2 * d_att, d_att).astype(BF16)
    kt_ref[...] = _dot_nt(w_kvt_ref[0:d_att, :], hb)
    vt_ref[...] = _dot_nt(w_kvt_ref[d_att:2 * d_att, :], hb)
    ca = proj(3 * d_att, d_conv)
    cb = proj(3 * d_att + d_conv, d_conv)
    u_ref[...] = ca * _sigmoid(cb)
    ga_ref[...] = _sigmoid(proj(3 * d_att + 2 * d_conv, d_model))
    gc_ref[...] = _sigmoid(proj(3 * d_att + 2 * d_conv + d_model, d_model))


def _mod_spec(per_token, tm, d, rows_per_seq):
    if per_token:
        return pl.BlockSpec((tm, d), lambda i: (i, 0))
    tiles_per_seq = rows_per_seq // tm
    return pl.BlockSpec((None, 1, d), lambda i: (i // tiles_per_seq, 0, 0))


def _inproj(x, sc, sh, g, w_in, w_kvt, *, d_att, d_conv, per_token, rows_per_seq, kv_group):
    rows, d = x.shape
    tm = min(TOKEN_TILE, rows)
    head_dim = d_att // N_HEADS
    tiles_per_group = kv_group // tm
    row_spec = lambda width: pl.BlockSpec((tm, width), lambda i: (i, 0))
    row_out = lambda width, dtype: (row_spec(width), jax.ShapeDtypeStruct((rows, width), dtype))
    kvt_out = (pl.BlockSpec((None, d_att, tm), lambda i: (i // tiles_per_group, 0, i % tiles_per_group)),
               jax.ShapeDtypeStruct((rows // kv_group, d_att, kv_group), F32))
    mod_spec = _mod_spec(per_token, tm, d, rows_per_seq)
    outs = (row_out(d_att, BF16), kvt_out, kvt_out, row_out(d_att, BF16), row_out(d_att, BF16),
            row_out(d_conv, F32), row_out(d, F32), row_out(d, F32))
    return pl.pallas_call(
        functools.partial(_inproj_body, d_att=d_att, d_conv=d_conv, q_scale=head_dim ** -0.5),
        grid=(rows // tm,),
        in_specs=[row_spec(d), mod_spec, mod_spec,
                  pl.BlockSpec((1, d), lambda i: (0, 0)),
                  pl.BlockSpec(w_in.shape, lambda i: (0, 0)),
                  pl.BlockSpec(w_kvt.shape, lambda i: (0, 0))],
        out_specs=[spec for spec, _ in outs],
        out_shape=[shape for _, shape in outs],
        compiler_params=_params(1),
        name="inproj",
    )(x, sc, sh, g.reshape(1, d), w_in, w_kvt)


def _suffix_matrix(tile):
    row = lax.broadcasted_iota(jnp.int32, (tile, tile), 0)
    col = lax.broadcasted_iota(jnp.int32, (tile, tile), 1)
    tri = (row >= col).astype(BF16)
    return jnp.concatenate([tri, tri], axis=0)


def _sb_suffix(z, tri2, valid):
    neg_abs = pltpu.bitcast(pltpu.bitcast(z, jnp.uint32) | jnp.uint32(0x80000000), F32)
    sp = jnp.maximum(z, 0.0) + jnp.log(1.0 + jnp.exp(neg_abs))
    if valid is not None:
        sp = jnp.where(valid, sp, 0.0)
    hi = sp.astype(BF16)
    lo = (sp - hi.astype(F32)).astype(BF16)
    return _dot(jnp.concatenate([hi, lo], axis=1), tri2)


def _sb_weights(z, right, valid):
    p = jnp.exp(z - right)
    if valid is not None:
        p = jnp.where(valid, p, 0.0)
    return p.astype(BF16)


def _sb_prompt_body(bias_ref, q_ref, k_ref, v_ref, o_ref,
                    qa_ref, z_ref, p_ref, acc_ref, ucur_ref, uprev_ref, *, tile, head_dim):
    hp = pl.program_id(1)
    qi = pl.program_id(2)
    pair = 2 * head_dim
    lane = lax.broadcasted_iota(jnp.int32, (1, pair), 1)
    row = lax.broadcasted_iota(jnp.int32, (2 * tile, tile), 0)
    col = lax.broadcasted_iota(jnp.int32, (2 * tile, tile), 1)
    causal = col < jnp.where(row >= tile, row - tile, row)
    tri2 = _suffix_matrix(tile)
    ones = jnp.ones((tile, pair), BF16)

    q = q_ref[0]
    for h in range(2):
        qa_ref[h * tile:(h + 1) * tile, 0:pair] = jnp.where(
            (lane >= h * head_dim) & (lane < (h + 1) * head_dim), q, jnp.zeros_like(q))
        b = bias_ref[2 * hp + h]
        b1 = b.astype(BF16).astype(F32)
        b2 = (b - b1).astype(BF16).astype(F32)
        b3 = (b - b1 - b2).astype(BF16).astype(F32)
        offset = jnp.where(lane == 0, b1, jnp.where(lane == 1, b2, jnp.where(lane == 2, b3, 0.0)))
        qa_ref[h * tile:(h + 1) * tile, pair:2 * pair] = jnp.broadcast_to(offset, (tile, pair)).astype(BF16)

    def logits(t, slot):
        off = pl.multiple_of(jnp.maximum(qi - t, 0) * tile, tile)
        ka = jnp.concatenate([k_ref[0, pl.ds(off, tile), :], ones], axis=1)
        z_ref[slot] = _dot_nt(qa_ref[...], ka)

    def weights(slot, valid):
        right = _sb_suffix(z_ref[slot], tri2, valid)
        p_ref[slot] = _sb_weights(z_ref[slot], right, valid)
        return right[:, 0:1]

    def values(t, slot):
        off = pl.multiple_of((qi - t) * tile, tile)
        acc_ref[...] += jnp.exp(-uprev_ref[...]) * _dot(p_ref[slot], v_ref[0, pl.ds(off, tile), :])

    logits(0, 0)
    logits(1, 1)
    ucur_ref[...] = weights(0, causal)
    uprev_ref[...] = jnp.zeros_like(uprev_ref)
    acc_ref[...] = jnp.zeros_like(acc_ref)

    def stage(t, slot):
        values(t - 1, 1 - slot)
        total = weights(slot, None)
        logits(t + 1, 1 - slot)
        ucur = ucur_ref[...]
        uprev_ref[...] = ucur
        ucur_ref[...] = ucur + total

    def two_stages(u, carry):
        stage(1 + 2 * u, 1)
        stage(2 + 2 * u, 0)
        return carry

    lax.fori_loop(0, qi // 2, two_stages, 0)

    @pl.when(qi % 2 == 1)
    def _():
        stage(qi, 1)
        values(qi, 1)

    @pl.when(qi % 2 == 0)
    def _():
        values(qi, 0)

    o_ref[0] = jnp.where(lane < head_dim, acc_ref[0:tile], acc_ref[tile:2 * tile]).astype(o_ref.dtype)


def _sb_attention_prompt(q, k, v, bias):
    n, s, d_att = q.shape
    head_dim = d_att // N_HEADS
    tile = min(ATT_TILE, s)
    pair = 2 * head_dim
    grid_spec = pltpu.PrefetchScalarGridSpec(
        num_scalar_prefetch=1,
        grid=(n, N_HEADS // 2, s // tile),
        in_specs=[pl.BlockSpec((1, tile, pair), lambda b, hp, qi, bias: (b, qi, hp)),
                  pl.BlockSpec((1, s, pair), lambda b, hp, qi, bias: (b, 0, hp)),
                  pl.BlockSpec((1, s, pair), lambda b, hp, qi, bias: (b, 0, hp))],
        out_specs=pl.BlockSpec((1, tile, pair), lambda b, hp, qi, bias: (b, qi, hp)),
        scratch_shapes=[pltpu.VMEM((2 * tile, 2 * pair), BF16),
                        pltpu.VMEM((2, 2 * tile, tile), F32),
                        pltpu.VMEM((2, 2 * tile, tile), BF16),
                        pltpu.VMEM((2 * tile, pair), F32),
                        pltpu.VMEM((2 * tile, 1), F32),
                        pltpu.VMEM((2 * tile, 1), F32)])
    return pl.pallas_call(
        functools.partial(_sb_prompt_body, tile=tile, head_dim=head_dim),
        grid_spec=grid_spec,
        out_shape=jax.ShapeDtypeStruct((n, s, d_att), BF16),
        compiler_params=_params(3),
        name="sb_prompt",
    )(bias, q, k, v)


def _sb_sample_body(pt_ref, qbd_ref, bias_ref, knew_ref, vnew_ref, *rest, page, t_new, n_steps):
    pages = rest[:2 * PAGES_PER_STEP]
    o_ref, kt_ref, vt_ref, acc_ref, used_ref = rest[2 * PAGES_PER_STEP:]
    g = pl.program_id(1)
    rows, d_att = acc_ref.shape
    head_dim = d_att // N_HEADS
    tri2 = _suffix_matrix(page)
    qbd = qbd_ref[0]
    bias = bias_ref[...]

    @pl.when(g == 0)
    def _():
        qidx = lax.broadcasted_iota(jnp.int32, (rows, page), 0) // N_HEADS
        kidx = lax.broadcasted_iota(jnp.int32, (rows, page), 1)
        valid = kidx < qidx
        z = _dot(qbd, knew_ref[0].astype(BF16)) + bias
        right = _sb_suffix(z, tri2, valid)
        acc_ref[...] = _dot_nt(_sb_weights(z, right, valid), vnew_ref[0].astype(BF16))
        used_ref[...] = right[:, 0:1]

    for i in range(PAGES_PER_STEP):
        kt_ref[:, i * page:(i + 1) * page] = pages[2 * i][0].astype(BF16)
        vt_ref[:, i * page:(i + 1) * page] = pages[2 * i + 1][0].astype(BF16)
    z_wide = _dot(qbd, kt_ref[...])
    z = jnp.concatenate([z_wide[:, i * page:(i + 1) * page] + bias for i in range(PAGES_PER_STEP)], axis=0)
    right = _sb_suffix(z, tri2, None)
    used = used_ref[...]
    used_rows = []
    for i in range(PAGES_PER_STEP):
        used_rows.append(used)
        used = used + right[i * rows:(i + 1) * rows, 0:1]
    used_ref[...] = used
    p = _sb_weights(z, right + jnp.concatenate(used_rows, axis=0), None)
    p_wide = jnp.concatenate([p[i * rows:(i + 1) * rows] for i in range(PAGES_PER_STEP)], axis=1)
    acc_ref[...] += _dot_nt(p_wide, vt_ref[...])

    @pl.when(g == n_steps - 1)
    def _():
        head = lax.broadcasted_iota(jnp.int32, (rows, d_att), 0) % N_HEADS
        lane_head = lax.broadcasted_iota(jnp.int32, (rows, d_att), 1) // head_dim
        own = jnp.where(head == lane_head, acc_ref[...], 0.0)
        o_ref[0] = jnp.sum(own.reshape(t_new, N_HEADS, d_att), axis=1)


def _sb_attention_sample(q, kt_new, vt_new, cache_k, cache_v, page_table, bias):
    nb, t_new, d_att = q.shape
    head_dim = d_att // N_HEADS
    page = cache_k.shape[2]
    n_pages = page_table.shape[1]
    n_steps = n_pages // PAGES_PER_STEP
    rows = t_new * N_HEADS
    head_mask = (jnp.arange(d_att)[None, :] // head_dim == jnp.arange(N_HEADS)[:, None]).astype(q.dtype)
    qbd = (q[:, :, None, :] * head_mask[None, None]).reshape(nb, rows, d_att)
    bias_rows = jnp.broadcast_to(jnp.tile(bias.astype(F32), t_new)[:, None], (rows, page))
    new_keys = lambda a: jnp.pad(a, ((0, 0), (0, 0), (0, page - t_new)))

    def page_spec(i):
        def index(b, g, pt):
            return (pt[b, n_pages - 1 - (g * PAGES_PER_STEP + i)], 0, 0)
        return pl.BlockSpec((1, d_att, page), index)

    page_specs, page_args = [], []
    for i in range(PAGES_PER_STEP):
        page_specs += [page_spec(i), page_spec(i)]
        page_args += [cache_k, cache_v]
    seq_spec = lambda r, c: pl.BlockSpec((1, r, c), lambda b, g, pt: (b, 0, 0))
    grid_spec = pltpu.PrefetchScalarGridSpec(
        num_scalar_prefetch=1,
        grid=(nb, n_steps),
        in_specs=[seq_spec(rows, d_att), pl.BlockSpec((rows, page), lambda b, g, pt: (0, 0)),
                  seq_spec(d_att, page), seq_spec(d_att, page)] + page_specs,
        out_specs=seq_spec(t_new, d_att),
        scratch_shapes=[pltpu.VMEM((d_att, PAGES_PER_STEP * page), BF16),
                        pltpu.VMEM((d_att, PAGES_PER_STEP * page), BF16),
                        pltpu.VMEM((rows, d_att), F32), pltpu.VMEM((rows, 1), F32)])
    return pl.pallas_call(
        functools.partial(_sb_sample_body, page=page, t_new=t_new, n_steps=n_steps),
        grid_spec=grid_spec,
        out_shape=jax.ShapeDtypeStruct((nb, t_new, d_att), F32),
        compiler_params=_params(2),
        name="sb_sample",
    )(page_table, qbd, bias_rows, new_keys(kt_new), new_keys(vt_new), *page_args)


def _conv_body(prev_ref, cur_ref, w_ref, b_ref, o_ref, ext_ref, *, width, tiles_per_seq, chunk):
    bs, tm, _ = cur_ref.shape
    prev = prev_ref[...]
    if tiles_per_seq is not None:
        prev = jnp.where(pl.program_id(0) % tiles_per_seq == 0, 0.0, prev)
    ext_ref[:, 0:HALO, :] = prev
    ext_ref[:, HALO:, :] = cur_ref[...]
    first = HALO - (width - 1)
    for b in range(bs):
        for r0 in range(0, tm, chunk):
            acc = ext_ref[b, first + r0:first + r0 + chunk, :] * w_ref[0:1, :]
            for w in range(1, width):
                acc += ext_ref[b, first + r0 + w:first + r0 + w + chunk, :] * w_ref[w:w + 1, :]
            o_ref[b, r0:r0 + chunk, :] = acc + b_ref[...]


def _depthwise_conv(prev, cur, w_dw, b_dw, *, bs, tm, tiles_per_seq):
    n, rows, c = cur.shape
    width = w_dw.shape[0]
    w_pad = jnp.pad(w_dw, ((0, HALO - width), (0, 0)))
    if tiles_per_seq is None:
        grid = (n // bs,)
        prev_index = lambda i: (i, 0, 0)
        cur_index = lambda i: (i, 0, 0)
    else:
        grid = (n * tiles_per_seq,)
        halo_per_tile = tm // HALO
        prev_index = lambda i: (jnp.maximum(i * halo_per_tile - 1, 0), 0, 0)
        cur_index = lambda i: (i // tiles_per_seq, i % tiles_per_seq, 0)
    return pl.pallas_call(
        functools.partial(_conv_body, width=width, tiles_per_seq=tiles_per_seq, chunk=min(64, tm)),
        grid=grid,
        in_specs=[pl.BlockSpec((bs, HALO, c), prev_index),
                  pl.BlockSpec((bs, tm, c), cur_index),
                  pl.BlockSpec((HALO, c), lambda i: (0, 0)),
                  pl.BlockSpec((1, c), lambda i: (0, 0))],
        out_specs=pl.BlockSpec((bs, tm, c), cur_index),
        out_shape=jax.ShapeDtypeStruct((n, rows, c), F32),
        scratch_shapes=[pltpu.VMEM((bs, HALO + tm, c), F32)],
        compiler_params=_params(1),
        name="conv",
    )(prev, cur, w_pad, b_dw.reshape(1, c))


def _merge_body(att_ref, dw_ref, ga_ref, gc_ref, x_ref, gt1_ref, sc2_ref, sh2_ref,
                w_att_ref, ln_g_ref, ln_b_ref, w_conv_ref, b_conv_ref, w_out_ref,
                g_post_ref, g_pre_ref, w_router_ref, b_router_ref,
                x1_ref, h2_ref, gate_ref, expert_ref):
    y_att = _dot(att_ref[...].astype(BF16), w_att_ref[...])
    dw = dw_ref[...]
    mu = jnp.mean(dw, axis=-1, keepdims=True)
    xc = dw - mu
    ln = xc * lax.rsqrt(jnp.mean(xc * xc, axis=-1, keepdims=True) + LN_EPS) * ln_g_ref[...] + ln_b_ref[...]
    act = ln * _sigmoid(ln)
    y_conv = _dot(act.astype(BF16), w_conv_ref[...]) + b_conv_ref[...]
    mixed = ga_ref[...] * y_att + gc_ref[...] * y_conv
    out = _dot(mixed.astype(BF16), w_out_ref[...])
    x1 = x_ref[...] + gt1_ref[...] * _rms(out, g_post_ref[...])
    x1_ref[...] = x1
    h2 = _rms(x1, g_pre_ref[...]) * (1.0 + sc2_ref[...]) + sh2_ref[...]
    h2_ref[...] = h2
    logits = _dot(h2.astype(BF16), w_router_ref[...]) + b_router_ref[...]

    n_exp = logits.shape[1]
    lane = lax.broadcasted_iota(jnp.int32, logits.shape, 1)
    out_lane = lax.broadcasted_iota(jnp.int32, gate_ref.shape, 1)
    experts = jnp.zeros(expert_ref.shape, jnp.int32)
    weights = jnp.zeros(gate_ref.shape, F32)
    top = None
    for k in range(TOP_K):
        best = jnp.max(logits, axis=-1, keepdims=True)
        idx = jnp.min(jnp.where(logits == best, lane, n_exp), axis=-1, keepdims=True)
        top = best if top is None else top
        experts = jnp.where(out_lane == k, idx, experts)
        weights = jnp.where(out_lane == k, jnp.exp(best - top), weights)
        logits = jnp.where(lane == idx, -jnp.inf, logits)
    gate_ref[...] = weights / jnp.sum(weights, axis=-1, keepdims=True)
    expert_ref[...] = experts


def _merge(att, dw, ga, gc, x, gt1, sc2, sh2, weights, *, per_token, rows_per_seq):
    rows, d = x.shape
    tm = min(TOKEN_TILE, rows)
    row_spec = lambda a: pl.BlockSpec((tm, a.shape[1]), lambda i: (i, 0))
    mod_spec = _mod_spec(per_token, tm, d, rows_per_seq)
    full_spec = lambda a: pl.BlockSpec(a.shape, lambda i: (0, 0))
    out_widths = (d, d, LANES, LANES)
    out_dtypes = (F32, F32, F32, jnp.int32)
    return pl.pallas_call(
        _merge_body,
        grid=(rows // tm,),
        in_specs=[row_spec(a) for a in (att, dw, ga, gc, x)] + [mod_spec] * 3 + [full_spec(w) for w in weights],
        out_specs=[pl.BlockSpec((tm, w), lambda i: (i, 0)) for w in out_widths],
        out_shape=[jax.ShapeDtypeStruct((rows, w), t) for w, t in zip(out_widths, out_dtypes)],
        compiler_params=_params(1),
        name="merge",
    )(att, dw, ga, gc, x, gt1, sc2, sh2, *weights)


def _to_bf16_body(w_ref, o_ref):
    o_ref[...] = w_ref[...].astype(BF16)


def _to_bf16(w):
    n_exp, rows, cols = w.shape
    tr = min(512, rows)
    spec = pl.BlockSpec((None, tr, cols), lambda e, r: (e, r, 0))
    return pl.pallas_call(
        _to_bf16_body,
        grid=(n_exp, rows // tr),
        in_specs=[spec],
        out_specs=spec,
        out_shape=jax.ShapeDtypeStruct(w.shape, BF16),
        compiler_params=_params(2),
        name="to_bf16",
    )(w)


def _moe_body(tile_expert_ref, n_tiles_ref, x_ref, w_gu_ref, b_gu_ref, w_down_ref, b_down_ref, o_ref, *, chunk):
    d_expert = w_down_ref.shape[0]

    @pl.when(pl.program_id(0) < n_tiles_ref[0])
    def _():
        x = x_ref[...].astype(BF16)
        acc = None
        for c0 in range(0, d_expert, chunk):
            g = _dot(x, w_gu_ref[:, c0:c0 + chunk]) + b_gu_ref[:, c0:c0 + chunk]
            u = _dot(x, w_gu_ref[:, d_expert + c0:d_expert + c0 + chunk]) + b_gu_ref[:, d_expert + c0:d_expert + c0 + chunk]
            g = jnp.minimum(g, SWIGLU_LIMIT)
            u = jnp.clip(u, -SWIGLU_LIMIT, SWIGLU_LIMIT)
            a = ((u + 1.0) * g * _sigmoid(SWIGLU_ALPHA * g)).astype(BF16)
            part = _dot(a, w_down_ref[c0:c0 + chunk, :])
            acc = part if acc is None else acc + part
        o_ref[...] = acc + b_down_ref[...]

    @pl.when(pl.program_id(0) >= n_tiles_ref[0])
    def _():
        o_ref[...] = jnp.zeros_like(o_ref)


def _moe_experts(x_sorted, tile_expert, n_tiles, w_gu, b_gu, w_down, b_down):
    n_rows, d = x_sorted.shape
    n_exp, _, d_gu = w_gu.shape
    d_expert = w_down.shape[1]
    max_tiles = n_rows // MOE_TILE

    def row_index(i, te, nt):
        return (jnp.minimum(i, nt[0] - 1), 0)

    def expert_index(i, te, nt):
        return (te[i], 0, 0)

    grid_spec = pltpu.PrefetchScalarGridSpec(
        num_scalar_prefetch=2,
        grid=(max_tiles,),
        in_specs=[pl.BlockSpec((MOE_TILE, d), row_index),
                  pl.BlockSpec((None, d, d_gu), expert_index),
                  pl.BlockSpec((None, 1, d_gu), expert_index),
                  pl.BlockSpec((None, d_expert, d), expert_index),
                  pl.BlockSpec((None, 1, d), expert_index)],
        out_specs=pl.BlockSpec((MOE_TILE, d), lambda i, te, nt: (i, 0)))
    return pl.pallas_call(
        functools.partial(_moe_body, chunk=min(512, d_expert)),
        grid_spec=grid_spec,
        out_shape=jax.ShapeDtypeStruct((n_rows, d), F32),
        compiler_params=_params(1),
        name="moe",
    )(tile_expert, n_tiles, x_sorted, w_gu, b_gu.reshape(n_exp, 1, d_gu), w_down, b_down.reshape(n_exp, 1, d))


def _rank_body(expert_ref, rank_ref, count_ref, carry_ref):
    i = pl.program_id(0)
    tm, lanes = expert_ref.shape

    @pl.when(i == 0)
    def _():
        carry_ref[...] = jnp.zeros_like(carry_ref)

    experts = expert_ref[...]
    lane = lax.broadcasted_iota(jnp.int32, (tm, lanes), 1)
    picks = [lane == experts[:, k:k + 1] for k in range(TOP_K)]
    chosen = functools.reduce(jnp.logical_or, picks)
    row = lax.broadcasted_iota(jnp.int32, (tm, tm), 0)
    col = lax.broadcasted_iota(jnp.int32, (tm, tm), 1)
    earlier = (col < row).astype(BF16)
    before = _dot(earlier, jnp.where(chosen, 1.0, 0.0).astype(BF16)) + carry_ref[...]
    rank = jnp.zeros((tm, lanes), F32)
    for k in range(TOP_K):
        rank = jnp.where(lane == k, jnp.sum(jnp.where(picks[k], before, 0.0), axis=-1, keepdims=True), rank)
    rank_ref[...] = rank.astype(jnp.int32)
    carry_ref[...] += jnp.sum(jnp.where(chosen, 1.0, 0.0), axis=0, keepdims=True)
    count_ref[...] = carry_ref[...].astype(jnp.int32)


def _rank(experts):
    n_tok, lanes = experts.shape
    tm = TOKEN_TILE
    return pl.pallas_call(
        _rank_body,
        grid=(n_tok // tm,),
        in_specs=[pl.BlockSpec((tm, lanes), lambda i: (i, 0))],
        out_specs=[pl.BlockSpec((tm, lanes), lambda i: (i, 0)), pl.BlockSpec((1, lanes), lambda i: (0, 0))],
        out_shape=[jax.ShapeDtypeStruct((n_tok, lanes), jnp.int32), jax.ShapeDtypeStruct((1, lanes), jnp.int32)],
        scratch_shapes=[pltpu.VMEM((1, lanes), F32)],
        compiler_params=_params(1),
        name="rank",
    )(experts)


def _layout(counts, n_assign):
    n_exp = counts.shape[0]
    padded = (counts + MOE_TILE - 1) // MOE_TILE * MOE_TILE
    pend = jnp.cumsum(padded)
    max_tiles = (n_assign + MOE_TILE - 1) // MOE_TILE + n_exp
    n_tiles = (pend[-1] // MOE_TILE).astype(jnp.int32)
    tile_id = jnp.minimum(jnp.arange(max_tiles, dtype=jnp.int32), n_tiles - 1)
    tile_expert = jnp.minimum(jnp.searchsorted(pend, tile_id * MOE_TILE, side="right"), n_exp - 1).astype(jnp.int32)
    return pend - padded, tile_expert, n_tiles.reshape(1), max_tiles * MOE_TILE


def _row_copy(src_ref, src_row, dst_ref, dst_row, sem):
    return pltpu.make_async_copy(src_ref.at[pl.ds(src_row, 1)], dst_ref.at[pl.ds(dst_row, 1)], sem)


def _dispatch_body(pos_ref, h_ref, init_ref, o_ref, sem):
    del init_ref
    tm = h_ref.shape[0]

    def issue(t, carry):
        for k in range(TOP_K):
            _row_copy(h_ref, t, o_ref, pos_ref[0, 0, t * TOP_K + k], sem).start()
        return carry

    lax.fori_loop(0, tm, issue, 0)

    for k in range(TOP_K):
        pltpu.make_async_copy(h_ref, o_ref.at[pl.ds(0, tm)], sem).wait()


def _dispatch(h, pos, buffer):
    n_tok, d = h.shape
    tm = min(TOKEN_TILE, n_tok)
    return pl.pallas_call(
        _dispatch_body,
        grid=(n_tok // tm,),
        in_specs=[pl.BlockSpec((1, 1, tm * TOP_K), lambda i: (i, 0, 0), memory_space=pltpu.SMEM),
                  pl.BlockSpec((tm, d), lambda i: (i, 0)),
                  pl.BlockSpec(memory_space=pl.ANY)],
        out_specs=pl.BlockSpec(memory_space=pl.ANY),
        out_shape=jax.ShapeDtypeStruct(buffer.shape, buffer.dtype),
        scratch_shapes=[pltpu.SemaphoreType.DMA(())],
        input_output_aliases={2: 0},
        compiler_params=_params(1),
        name="dispatch",
    )(pos.reshape(n_tok // tm, 1, tm * TOP_K), h, buffer)


def _combine_body(pos_ref, y_ref, gate_ref, x1_ref, gt2_ref, g_ref, o_ref, buf_ref, sem):
    tm = x1_ref.shape[0]

    def issue(t, carry):
        for k in range(TOP_K):
            _row_copy(y_ref, pos_ref[0, 0, t * TOP_K + k], buf_ref.at[k], t, sem).start()
        return carry

    lax.fori_loop(0, tm, issue, 0)

    for k in range(TOP_K):
        pltpu.make_async_copy(y_ref.at[pl.ds(0, tm)], buf_ref.at[k], sem).wait()
    gates = gate_ref[...]
    y = gates[:, 0:1] * buf_ref[0]
    for k in range(1, TOP_K):
        y += gates[:, k:k + 1] * buf_ref[k]
    o_ref[...] = x1_ref[...] + gt2_ref[...] * _rms(y, g_ref[...])


def _combine(y_sorted, pos, gates, x1, gt2, g, *, per_token, rows_per_seq):
    rows, d = x1.shape
    tm = min(TOKEN_TILE, rows)
    row_spec = lambda w: pl.BlockSpec((tm, w), lambda i: (i, 0))
    return pl.pallas_call(
        _combine_body,
        grid=(rows // tm,),
        in_specs=[pl.BlockSpec((1, 1, tm * TOP_K), lambda i: (i, 0, 0), memory_space=pltpu.SMEM),
                  pl.BlockSpec(memory_space=pl.ANY),
                  row_spec(gates.shape[1]), row_spec(d), _mod_spec(per_token, tm, d, rows_per_seq),
                  pl.BlockSpec((1, d), lambda i: (0, 0))],
        out_specs=row_spec(d),
        out_shape=jax.ShapeDtypeStruct((rows, d), F32),
        scratch_shapes=[pltpu.VMEM((TOP_K, tm, d), F32), pltpu.SemaphoreType.DMA(())],
        compiler_params=_params(1),
        name="combine",
    )(pos.reshape(rows // tm, 1, tm * TOP_K), y_sorted, gates, x1, gt2, g.reshape(1, d))


def kernel(x_prompt, x_sample, cache_k, cache_v, state_conv, page_table, c_prompt, c_sample, w_mod, b_mod, g_pre_mix, g_post_mix, w_in, b_sb, w_att_out, w_dw, b_dw, ln_conv_g, ln_conv_b, w_conv_out, b_conv_out, w_out, g_pre_ffn, g_post_ffn, w_router, b_router, w_gu, b_gu, w_down, b_down):
    depth = w_mod.shape[0]
    assert depth == 1, "single-layer trunk"
    n, s, d = x_prompt.shape
    nb, t_new, _ = x_sample.shape
    d_att = w_att_out.shape[1]
    d_conv = w_dw.shape[2]
    width = w_dw.shape[1]
    n_exp = w_router.shape[2]
    head_dim = d_att // N_HEADS
    page = cache_k.shape[2]
    assert width - 1 <= HALO and s % TOKEN_TILE == 0
    l = 0

    row2 = lambda a: a.reshape(1, -1)
    w_in_b = w_in[l].astype(BF16)
    w_kvt = w_in_b[:, d_att:3 * d_att].T
    merge_w = (w_att_out[l].astype(BF16), row2(ln_conv_g[l]), row2(ln_conv_b[l]),
               w_conv_out[l].astype(BF16), row2(b_conv_out[l]), w_out[l].astype(BF16),
               row2(g_post_mix[l]), row2(g_pre_ffn[l]), w_router[l].astype(BF16), row2(b_router[l]))

    n_seq = n + nb
    c_all = jnp.pad(jnp.concatenate([c_prompt, c_sample], axis=0), ((0, -n_seq % 8), (0, 0)))
    mod = _modulation(c_all, w_mod[l], b_mod[l])
    mod_p = [m.reshape(n, 1, d) for m in jnp.split(mod[:n], 6, axis=-1)]
    mod_s = jnp.split(jnp.repeat(mod[n:n_seq], t_new, axis=0), 6, axis=-1)

    def mixing(x, mods, attend, conv, per_token, rows_per_seq, kv_group):
        sh1, sc1, gt1, sh2, sc2, _ = mods
        q, kt, vt, kb, vb, u, ga, gc = _inproj(x, sc1, sh1, g_pre_mix[l], w_in_b, w_kvt, d_att=d_att, d_conv=d_conv,
                                               per_token=per_token, rows_per_seq=rows_per_seq, kv_group=kv_group)
        att = attend(q, kt, vt, kb, vb)
        dw = conv(u)
        x1, h2, gates, experts = _merge(att, dw, ga, gc, x, gt1, sc2, sh2, merge_w,
                                        per_token=per_token, rows_per_seq=rows_per_seq)
        return x1, h2, gates, experts, kt, vt, u

    def attend_prompt(q, kt, vt, kb, vb):
        shape = (n, s, d_att)
        return _sb_attention_prompt(q.reshape(shape), kb.reshape(shape), vb.reshape(shape),
                                    b_sb[l].astype(F32)).reshape(n * s, d_att)

    def conv_prompt(u):
        tm = min(512, s)
        dw = _depthwise_conv(u.reshape(n * s // HALO, HALO, d_conv), u.reshape(n, s, d_conv), w_dw[l], b_dw[l],
                             bs=1, tm=tm, tiles_per_seq=s // tm)
        return dw.reshape(n * s, d_conv)

    x1_p, h2_p, gates_p, experts_p, kt_p, vt_p, u_p = mixing(x_prompt.reshape(n * s, d), mod_p, attend_prompt,
                                                             conv_prompt, False, s, s)

    pool_k = cache_k[l].transpose(0, 2, 3, 1).reshape(-1, d_att, page)
    pool_v = cache_v[l].transpose(0, 2, 3, 1).reshape(-1, d_att, page)

    def attend_sample(q, kt, vt, kb, vb):
        per_seq = lambda a: a.reshape(d_att, nb, t_new).transpose(1, 0, 2)
        return _sb_attention_sample(q.reshape(nb, t_new, d_att), per_seq(kt), per_seq(vt), pool_k, pool_v,
                                    page_table, b_sb[l]).reshape(nb * t_new, d_att)

    def conv_sample(u):
        t_pad = -t_new % 8
        cur = jnp.pad(u.reshape(nb, t_new, d_conv), ((0, 0), (0, t_pad), (0, 0)))
        prev = jnp.pad(state_conv[l], ((0, 0), (HALO - (width - 1), 0), (0, 0)))
        dw = _depthwise_conv(prev, cur, w_dw[l], b_dw[l], bs=min(16, nb), tm=t_new + t_pad, tiles_per_seq=None)
        return dw[:, :t_new].reshape(nb * t_new, d_conv)

    x1_s, h2_s, gates_s, experts_s, kt_s, vt_s, u_s = mixing(x_sample.reshape(nb * t_new, d), mod_s, attend_sample,
                                                             conv_sample, True, t_new, nb * t_new)

    n_p = n * s
    n_tok = n_p + nb * t_new
    assert n_p % TOKEN_TILE == 0 and n_tok % TOKEN_TILE == 0
    experts = jnp.concatenate([experts_p, experts_s], axis=0)
    rank, counts = _rank(experts)
    row_start, tile_expert, n_tiles, n_rows = _layout(counts[0, :n_exp], n_tok * TOP_K)
    pos = row_start[experts[:, :TOP_K]] + rank[:, :TOP_K]
    x_sorted = _dispatch(h2_p, pos[:n_p], jnp.zeros((n_rows, d), F32))
    x_sorted = _dispatch(h2_s, pos[n_p:], x_sorted)
    y_sorted = _moe_experts(x_sorted, tile_expert, n_tiles, _to_bf16(w_gu[l]), b_gu[l], _to_bf16(w_down[l]), b_down[l])
    y_prompt = _combine(y_sorted, pos[:n_p], gates_p, x1_p, mod_p[5], g_post_ffn[l], per_token=False, rows_per_seq=s)
    y_sample = _combine(y_sorted, pos[n_p:], gates_s, x1_s, mod_s[5], g_post_ffn[l], per_token=True,
                        rows_per_seq=t_new)

    heads_last = lambda a, seqs, toks: a.reshape(N_HEADS, head_dim, seqs, toks).transpose(2, 3, 0, 1)[None]
    prompt_kv = lambda a: heads_last(a.transpose(1, 0, 2), n, s)
    sample_kv = lambda a: heads_last(a[0], nb, t_new)
    keep = width - 1
    conv_p = u_p.reshape(n, s, d_conv)[:, s - keep:]
    conv_s = jnp.concatenate([state_conv[l], u_s.reshape(nb, t_new, d_conv)], axis=1)[:, -keep:]
    return (y_prompt.reshape(n, s, d), y_sample.reshape(nb, t_new, d),
            prompt_kv(kt_p), prompt_kv(vt_p), conv_p[None],
            sample_kv(kt_s), sample_kv(vt_s), conv_s[None])
```

```python
import functools

import jax
import jax.numpy as jnp
from jax import lax
from jax.experimental import pallas as pl
from jax.experimental.pallas import tpu as pltpu

F32 = jnp.float32
BF16 = jnp.bfloat16

N_HEADS = 8
TOP_K = 4
SWIGLU_LIMIT = 7.0
SWIGLU_ALPHA = 1.702
RMS_EPS = 1e-6
LN_EPS = 1e-5
SOFTPLUS_LINEAR_ABOVE = 30.0

VMEM_LIMIT_BYTES = 48 * 1024 * 1024
HALO = 32
TOKEN_TILE = 256
ATT_KEY_TILE = 256
ATT_QUERY_TILE = 512
MOE_TILE = 512
LANES = 128
SUBLANES = 8
PAGES_PER_STEP = 8


def _params(n_axes):
    return pltpu.CompilerParams(dimension_semantics=("arbitrary",) * n_axes,
                                vmem_limit_bytes=VMEM_LIMIT_BYTES)


def _dot(a, b):
    return jnp.dot(a, b, preferred_element_type=F32)


def _dot_nt(a, b):
    return lax.dot_general(a, b, (((1,), (1,)), ((), ())), preferred_element_type=F32)


def _rms(x, g):
    return x * lax.rsqrt(jnp.mean(x * x, axis=-1, keepdims=True) + RMS_EPS) * g


def _sigmoid(x):
    return 1.0 / (1.0 + jnp.exp(-x))


def _mod_body(c_ref, w_ref, b_ref, o_ref):
    c = c_ref[...]
    s = (c * _sigmoid(c)).astype(BF16)
    o_ref[...] = _dot(s, w_ref[...].astype(BF16)) + b_ref[...]


def _modulation(c, w_mod, b_mod):
    n, d = c.shape
    n_out = w_mod.shape[1]
    tn = n_out // 6
    return pl.pallas_call(
        _mod_body,
        grid=(n_out // tn,),
        in_specs=[pl.BlockSpec((n, d), lambda j: (0, 0)),
                  pl.BlockSpec((d, tn), lambda j: (0, j)),
                  pl.BlockSpec((1, tn), lambda j: (0, j))],
        out_specs=pl.BlockSpec((n, tn), lambda j: (0, j)),
        out_shape=jax.ShapeDtypeStruct((n, n_out), F32),
        compiler_params=_params(1),
        name="mod",
    )(c, w_mod, b_mod.reshape(1, n_out))


def _inproj_body(x_ref, sc_ref, sh_ref, g_ref, w_ref, w_kvt_ref,
                 q_ref, kt_ref, vt_ref, kb_ref, vb_ref, u_ref, ga_ref, gc_ref, *, d_att, d_conv, q_scale):
    d_model = x_ref.shape[-1]
    h = _rms(x_ref[...], g_ref[...]) * (1.0 + sc_ref[...]) + sh_ref[...]
    hb = h.astype(BF16)

    def proj(lo, width):
        return _dot(hb, w_ref[:, lo:lo + width])

    q_ref[...] = (proj(0, d_att) * q_scale).astype(BF16)
    kb_ref[...] = proj(d_att, d_att).astype(BF16)
    vb_ref[...] = proj(2 * d_att, d_att).astype(BF16)
    kt_ref[...] = _dot_nt(w_kvt_ref[0:d_att, :], hb)
    vt_ref[...] = _dot_nt(w_kvt_ref[d_att:2 * d_att, :], hb)
    ca = proj(3 * d_att, d_conv)
    cb = proj(3 * d_att + d_conv, d_conv)
    u_ref[...] = ca * _sigmoid(cb)
    ga_ref[...] = _sigmoid(proj(3 * d_att + 2 * d_conv, d_model))
    gc_ref[...] = _sigmoid(proj(3 * d_att + 2 * d_conv + d_model, d_model))


def _mod_spec(per_token, tm, d, rows_per_seq):
    if per_token:
        return pl.BlockSpec((tm, d), lambda i: (i, 0))
    tiles_per_seq = rows_per_seq // tm
    return pl.BlockSpec((None, 1, d), lambda i: (i // tiles_per_seq, 0, 0))


def _inproj(x, sc, sh, g, w_in, w_kvt, *, d_att, d_conv, per_token, rows_per_seq, kv_group):
    rows, d = x.shape
    tm = min(TOKEN_TILE, rows)
    head_dim = d_att // N_HEADS
    tiles_per_group = kv_group // tm
    row_spec = lambda width: pl.BlockSpec((tm, width), lambda i: (i, 0))
    row_out = lambda width, dtype: (row_spec(width), jax.ShapeDtypeStruct((rows, width), dtype))
    kvt_out = (pl.BlockSpec((None, d_att, tm), lambda i: (i // tiles_per_group, 0, i % tiles_per_group)),
               jax.ShapeDtypeStruct((rows // kv_group, d_att, kv_group), F32))
    mod_spec = _mod_spec(per_token, tm, d, rows_per_seq)
    outs = (row_out(d_att, BF16), kvt_out, kvt_out, row_out(d_att, BF16), row_out(d_att, BF16),
            row_out(d_conv, F32), row_out(d, F32), row_out(d, F32))
    return pl.pallas_call(
        functools.partial(_inproj_body, d_att=d_att, d_conv=d_conv, q_scale=head_dim ** -0.5),
        grid=(rows // tm,),
        in_specs=[row_spec(d), mod_spec, mod_spec,
                  pl.BlockSpec((1, d), lambda i: (0, 0)),
                  pl.BlockSpec(w_in.shape, lambda i: (0, 0)),
                  pl.BlockSpec(w_kvt.shape, lambda i: (0, 0))],
        out_specs=[spec for spec, _ in outs],
        out_shape=[shape for _, shape in outs],
        compiler_params=_params(1),
        name="inproj",
    )(x, sc, sh, g.reshape(1, d), w_in, w_kvt)


def _suffix_matrix(tile):
    row = lax.broadcasted_iota(jnp.int32, (tile, tile), 0)
    col = lax.broadcasted_iota(jnp.int32, (tile, tile), 1)
    tri = (row >= col).astype(BF16)
    return jnp.concatenate([tri, tri], axis=0)


def _sb_suffix(z, tri2, valid):
    sp = jnp.where(z > SOFTPLUS_LINEAR_ABOVE, z, jnp.log(1.0 + jnp.exp(z)))
    if valid is not None:
        sp = jnp.where(valid, sp, 0.0)
    hi = sp.astype(BF16)
    lo = (sp - hi.astype(F32)).astype(BF16)
    return _dot(jnp.concatenate([hi, lo], axis=1), tri2)


def _sb_weights(z, right, valid):
    p = jnp.exp(z - right)
    if valid is not None:
        p = jnp.where(valid, p, 0.0)
    return p.astype(BF16)


def _sb_prompt_body(bias_ref, q_ref, k_ref, v_ref, o_ref,
                    qa_ref, z_ref, p_ref, acc_ref, ucur_ref, uprev_ref, *, tq, tk, head_dim):
    hp = pl.program_id(1)
    qi = pl.program_id(2)
    pair = 2 * head_dim
    ratio = tq // tk
    last = (qi + 1) * ratio - 1
    lane = lax.broadcasted_iota(jnp.int32, (1, pair), 1)
    row = lax.broadcasted_iota(jnp.int32, (2 * tq, tk), 0)
    col = lax.broadcasted_iota(jnp.int32, (2 * tq, tk), 1)
    q_pos = jnp.where(row >= tq, row - tq, row)
    causal = [col + (ratio - 1 - t) * tk < q_pos for t in range(ratio)]
    tri2 = _suffix_matrix(tk)
    ones = jnp.ones((tk, pair), BF16)

    q = q_ref[0]
    for h in range(2):
        qa_ref[h * tq:(h + 1) * tq, 0:pair] = jnp.where(
            (lane >= h * head_dim) & (lane < (h + 1) * head_dim), q, jnp.zeros_like(q))
        b = bias_ref[2 * hp + h]
        b1 = b.astype(BF16).astype(F32)
        b2 = (b - b1).astype(BF16).astype(F32)
        b3 = (b - b1 - b2).astype(BF16).astype(F32)
        offset = jnp.where(lane == 0, b1, jnp.where(lane == 1, b2, jnp.where(lane == 2, b3, 0.0)))
        qa_ref[h * tq:(h + 1) * tq, pair:2 * pair] = jnp.broadcast_to(offset, (tq, pair)).astype(BF16)

    def logits(t, slot):
        off = pl.multiple_of(jnp.maximum(last - t, 0) * tk, tk)
        ka = jnp.concatenate([k_ref[0, pl.ds(off, tk), :], ones], axis=1)
        z_ref[slot] = _dot_nt(qa_ref[...], ka)

    def weights(slot, valid):
        right = _sb_suffix(z_ref[slot], tri2, valid)
        p_ref[slot] = _sb_weights(z_ref[slot], right, valid)
        return right[:, 0:1]

    def values(t, slot):
        off = pl.multiple_of((last - t) * tk, tk)
        acc_ref[...] += jnp.exp(-uprev_ref[...]) * _dot(p_ref[slot], v_ref[0, pl.ds(off, tk), :])

    def stage(t, slot, valid):
        values(t - 1, 1 - slot)
        total = weights(slot, valid)
        logits(t + 1, 1 - slot)
        ucur = ucur_ref[...]
        uprev_ref[...] = ucur
        ucur_ref[...] = ucur + total

    logits(0, 0)
    logits(1, 1)
    ucur_ref[...] = weights(0, causal[0])
    uprev_ref[...] = jnp.zeros_like(uprev_ref)
    acc_ref[...] = jnp.zeros_like(acc_ref)
    for t in range(1, ratio):
        stage(t, t % 2, causal[t])

    def two_stages(u, carry):
        stage(ratio + 2 * u, 0, None)
        stage(ratio + 2 * u + 1, 1, None)
        return carry

    lax.fori_loop(0, qi * (ratio // 2), two_stages, 0)
    values(last, 1)
    o_ref[0] = jnp.where(lane < head_dim, acc_ref[0:tq], acc_ref[tq:2 * tq]).astype(o_ref.dtype)


def _sb_attention_prompt(q, k, v, bias):
    n, s, d_att = q.shape
    head_dim = d_att // N_HEADS
    tk = ATT_KEY_TILE
    tq = ATT_QUERY_TILE
    assert s % tq == 0 and tq % (2 * tk) == 0
    pair = 2 * head_dim
    grid_spec = pltpu.PrefetchScalarGridSpec(
        num_scalar_prefetch=1,
        grid=(n, N_HEADS // 2, s // tq),
        in_specs=[pl.BlockSpec((1, tq, pair), lambda b, hp, qi, bias: (b, qi, hp)),
                  pl.BlockSpec((1, s, pair), lambda b, hp, qi, bias: (b, 0, hp)),
                  pl.BlockSpec((1, s, pair), lambda b, hp, qi, bias: (b, 0, hp))],
        out_specs=pl.BlockSpec((1, tq, pair), lambda b, hp, qi, bias: (b, qi, hp)),
        scratch_shapes=[pltpu.VMEM((2 * tq, 2 * pair), BF16),
                        pltpu.VMEM((2, 2 * tq, tk), F32),
                        pltpu.VMEM((2, 2 * tq, tk), BF16),
                        pltpu.VMEM((2 * tq, pair), F32),
                        pltpu.VMEM((2 * tq, 1), F32),
                        pltpu.VMEM((2 * tq, 1), F32)])
    return pl.pallas_call(
        functools.partial(_sb_prompt_body, tq=tq, tk=tk, head_dim=head_dim),
        grid_spec=grid_spec,
        out_shape=jax.ShapeDtypeStruct((n, s, d_att), BF16),
        compiler_params=_params(3),
        name="sb_prompt",
    )(bias, q, k, v)


def _sb_sample_body(pt_ref, qbd_ref, bias_ref, knew_ref, vnew_ref, *rest, page, t_new, n_steps):
    pages = rest[:2 * PAGES_PER_STEP]
    o_ref, kt_ref, vt_ref, acc_ref, used_ref = rest[2 * PAGES_PER_STEP:]
    g = pl.program_id(1)
    rows, d_att = acc_ref.shape
    head_dim = d_att // N_HEADS
    tri2 = _suffix_matrix(page)
    qbd = qbd_ref[0]
    bias = bias_ref[...]

    @pl.when(g == 0)
    def _():
        qidx = lax.broadcasted_iota(jnp.int32, (rows, page), 0) // N_HEADS
        kidx = lax.broadcasted_iota(jnp.int32, (rows, page), 1)
        valid = kidx < qidx
        z = _dot(qbd, knew_ref[0].astype(BF16)) + bias
        right = _sb_suffix(z, tri2, valid)
        acc_ref[...] = _dot_nt(_sb_weights(z, right, valid), vnew_ref[0].astype(BF16))
        used_ref[...] = right[:, 0:1]

    for i in range(PAGES_PER_STEP):
        kt_ref[:, i * page:(i + 1) * page] = pages[2 * i][0].astype(BF16)
        vt_ref[:, i * page:(i + 1) * page] = pages[2 * i + 1][0].astype(BF16)
    z_wide = _dot(qbd, kt_ref[...])
    z = jnp.concatenate([z_wide[:, i * page:(i + 1) * page] + bias for i in range(PAGES_PER_STEP)], axis=0)
    right = _sb_suffix(z, tri2, None)
    used = used_ref[...]
    used_rows = []
    for i in range(PAGES_PER_STEP):
        used_rows.append(used)
        used = used + right[i * rows:(i + 1) * rows, 0:1]
    used_ref[...] = used
    p = _sb_weights(z, right + jnp.concatenate(used_rows, axis=0), None)
    p_wide = jnp.concatenate([p[i * rows:(i + 1) * rows] for i in range(PAGES_PER_STEP)], axis=1)
    acc_ref[...] += _dot_nt(p_wide, vt_ref[...])

    @pl.when(g == n_steps - 1)
    def _():
        head = lax.broadcasted_iota(jnp.int32, (rows, d_att), 0) % N_HEADS
        lane_head = lax.broadcasted_iota(jnp.int32, (rows, d_att), 1) // head_dim
        own = jnp.where(head == lane_head, acc_ref[...], 0.0)
        o_ref[0] = jnp.sum(own.reshape(t_new, N_HEADS, d_att), axis=1)


def _sb_attention_sample(q, kt_new, vt_new, cache_k, cache_v, page_table, bias):
    nb, t_new, d_att = q.shape
    head_dim = d_att // N_HEADS
    page = cache_k.shape[2]
    n_pages = page_table.shape[1]
    n_steps = n_pages // PAGES_PER_STEP
    rows = t_new * N_HEADS
    head_mask = (jnp.arange(d_att)[None, :] // head_dim == jnp.arange(N_HEADS)[:, None]).astype(q.dtype)
    qbd = (q[:, :, None, :] * head_mask[None, None]).reshape(nb, rows, d_att)
    bias_rows = jnp.broadcast_to(jnp.tile(bias.astype(F32), t_new)[:, None], (rows, page))
    new_keys = lambda a: jnp.pad(a, ((0, 0), (0, 0), (0, page - t_new)))

    def page_spec(i):
        def index(b, g, pt):
            return (pt[b, n_pages - 1 - (g * PAGES_PER_STEP + i)], 0, 0)
        return pl.BlockSpec((1, d_att, page), index)

    page_specs, page_args = [], []
    for i in range(PAGES_PER_STEP):
        page_specs += [page_spec(i), page_spec(i)]
        page_args += [cache_k, cache_v]
    seq_spec = lambda r, c: pl.BlockSpec((1, r, c), lambda b, g, pt: (b, 0, 0))
    grid_spec = pltpu.PrefetchScalarGridSpec(
        num_scalar_prefetch=1,
        grid=(nb, n_steps),
        in_specs=[seq_spec(rows, d_att), pl.BlockSpec((rows, page), lambda b, g, pt: (0, 0)),
                  seq_spec(d_att, page), seq_spec(d_att, page)] + page_specs,
        out_specs=seq_spec(t_new, d_att),
        scratch_shapes=[pltpu.VMEM((d_att, PAGES_PER_STEP * page), BF16),
                        pltpu.VMEM((d_att, PAGES_PER_STEP * page), BF16),
                        pltpu.VMEM((rows, d_att), F32), pltpu.VMEM((rows, 1), F32)])
    return pl.pallas_call(
        functools.partial(_sb_sample_body, page=page, t_new=t_new, n_steps=n_steps),
        grid_spec=grid_spec,
        out_shape=jax.ShapeDtypeStruct((nb, t_new, d_att), F32),
        compiler_params=_params(2),
        name="sb_sample",
    )(page_table, qbd, bias_rows, new_keys(kt_new), new_keys(vt_new), *page_args)


def _conv_body(prev_ref, cur_ref, w_ref, b_ref, o_ref, ext_ref, *, width, tiles_per_seq, chunk):
    bs, tm, _ = cur_ref.shape
    span = ext_ref.shape[2]
    prev = prev_ref[...]
    if tiles_per_seq is not None:
        prev = jnp.where(pl.program_id(0) % tiles_per_seq == 0, 0.0, prev)
    ext_ref[0, :, 0:HALO, :] = prev
    ext_ref[0, :, HALO:HALO + tm, :] = cur_ref[...]
    for s in range(1, SUBLANES):
        ext_ref[s, :, 0:span - SUBLANES, :] = ext_ref[0, :, s:s + span - SUBLANES, :]
    first = HALO - (width - 1)
    for b in range(bs):
        for r0 in range(0, tm, chunk):
            acc = None
            for w in range(width):
                shift = (first + w) % SUBLANES
                base = first + w - shift + r0
                term = ext_ref[shift, b, base:base + chunk, :] * w_ref[w:w + 1, :]
                acc = term if acc is None else acc + term
            o_ref[b, r0:r0 + chunk, :] = acc + b_ref[...]


def _depthwise_conv(prev, cur, w_dw, b_dw, *, bs, tm, tiles_per_seq):
    n, rows, c = cur.shape
    width = w_dw.shape[0]
    w_pad = jnp.pad(w_dw, ((0, HALO - width), (0, 0)))
    if tiles_per_seq is None:
        grid = (n // bs,)
        prev_index = lambda i: (i, 0, 0)
        cur_index = lambda i: (i, 0, 0)
    else:
        grid = (n * tiles_per_seq,)
        halo_per_tile = tm // HALO
        prev_index = lambda i: (jnp.maximum(i * halo_per_tile - 1, 0), 0, 0)
        cur_index = lambda i: (i // tiles_per_seq, i % tiles_per_seq, 0)
    return pl.pallas_call(
        functools.partial(_conv_body, width=width, tiles_per_seq=tiles_per_seq, chunk=min(64, tm)),
        grid=grid,
        in_specs=[pl.BlockSpec((bs, HALO, c), prev_index),
                  pl.BlockSpec((bs, tm, c), cur_index),
                  pl.BlockSpec((HALO, c), lambda i: (0, 0)),
                  pl.BlockSpec((1, c), lambda i: (0, 0))],
        out_specs=pl.BlockSpec((bs, tm, c), cur_index),
        out_shape=jax.ShapeDtypeStruct((n, rows, c), F32),
        scratch_shapes=[pltpu.VMEM((SUBLANES, bs, HALO + tm, c), F32)],
        compiler_params=_params(1),
        name="conv",
    )(prev, cur, w_pad, b_dw.reshape(1, c))


def _merge_body(att_ref, dw_ref, ga_ref, gc_ref, x_ref, gt1_ref, sc2_ref, sh2_ref,
                w_att_ref, ln_g_ref, ln_b_ref, w_conv_ref, b_conv_ref, w_out_ref,
                g_post_ref, g_pre_ref, w_router_ref, b_router_ref,
                x1_ref, h2_ref, gate_ref, expert_ref):
    y_att = _dot(att_ref[...].astype(BF16), w_att_ref[...])
    dw = dw_ref[...]
    mu = jnp.mean(dw, axis=-1, keepdims=True)
    xc = dw - mu
    ln = xc * lax.rsqrt(jnp.mean(xc * xc, axis=-1, keepdims=True) + LN_EPS) * ln_g_ref[...] + ln_b_ref[...]
    act = ln * _sigmoid(ln)
    y_conv = _dot(act.astype(BF16), w_conv_ref[...]) + b_conv_ref[...]
    mixed = ga_ref[...] * y_att + gc_ref[...] * y_conv
    out = _dot(mixed.astype(BF16), w_out_ref[...])
    x1 = x_ref[...] + gt1_ref[...] * _rms(out, g_post_ref[...])
    x1_ref[...] = x1
    h2 = _rms(x1, g_pre_ref[...]) * (1.0 + sc2_ref[...]) + sh2_ref[...]
    h2_ref[...] = h2
    logits = _dot(h2.astype(BF16), w_router_ref[...]) + b_router_ref[...]

    n_exp = logits.shape[1]
    lane = lax.broadcasted_iota(jnp.int32, logits.shape, 1)
    out_lane = lax.broadcasted_iota(jnp.int32, gate_ref.shape, 1)
    experts = jnp.zeros(expert_ref.shape, jnp.int32)
    weights = jnp.zeros(gate_ref.shape, F32)
    top = None
    for k in range(TOP_K):
        best = jnp.max(logits, axis=-1, keepdims=True)
        idx = jnp.min(jnp.where(logits == best, lane, n_exp), axis=-1, keepdims=True)
        top = best if top is None else top
        experts = jnp.where(out_lane == k, idx, experts)
        weights = jnp.where(out_lane == k, jnp.exp(best - top), weights)
        logits = jnp.where(lane == idx, -jnp.inf, logits)
    gate_ref[...] = weights / jnp.sum(weights, axis=-1, keepdims=True)
    expert_ref[...] = experts


def _merge(att, dw, ga, gc, x, gt1, sc2, sh2, weights, *, per_token, rows_per_seq):
    rows, d = x.shape
    tm = min(TOKEN_TILE, rows)
    row_spec = lambda a: pl.BlockSpec((tm, a.shape[1]), lambda i: (i, 0))
    mod_spec = _mod_spec(per_token, tm, d, rows_per_seq)
    full_spec = lambda a: pl.BlockSpec(a.shape, lambda i: (0, 0))
    out_widths = (d, d, LANES, LANES)
    out_dtypes = (F32, F32, F32, jnp.int32)
    return pl.pallas_call(
        _merge_body,
        grid=(rows // tm,),
        in_specs=[row_spec(a) for a in (att, dw, ga, gc, x)] + [mod_spec] * 3 + [full_spec(w) for w in weights],
        out_specs=[pl.BlockSpec((tm, w), lambda i: (i, 0)) for w in out_widths],
        out_shape=[jax.ShapeDtypeStruct((rows, w), t) for w, t in zip(out_widths, out_dtypes)],
        compiler_params=_params(1),
        name="merge",
    )(att, dw, ga, gc, x, gt1, sc2, sh2, *weights)


def _to_bf16_body(w_ref, o_ref):
    o_ref[...] = w_ref[...].astype(BF16)


def _to_bf16(w):
    n_exp, rows, cols = w.shape
    tr = min(512, rows)
    spec = pl.BlockSpec((None, tr, cols), lambda e, r: (e, r, 0))
    return pl.pallas_call(
        _to_bf16_body,
        grid=(n_exp, rows // tr),
        in_specs=[spec],
        out_specs=spec,
        out_shape=jax.ShapeDtypeStruct(w.shape, BF16),
        compiler_params=_params(2),
        name="to_bf16",
    )(w)


def _moe_body(tile_expert_ref, n_tiles_ref, x_ref, w_gu_ref, b_gu_ref, w_down_ref, b_down_ref, o_ref, *, chunk):
    d_expert = w_down_ref.shape[0]

    @pl.when(pl.program_id(0) < n_tiles_ref[0])
    def _():
        x = x_ref[...].astype(BF16)
        acc = None
        for c0 in range(0, d_expert, chunk):
            g = _dot(x, w_gu_ref[:, c0:c0 + chunk]) + b_gu_ref[:, c0:c0 + chunk]
            u = _dot(x, w_gu_ref[:, d_expert + c0:d_expert + c0 + chunk]) + b_gu_ref[:, d_expert + c0:d_expert + c0 + chunk]
            g = jnp.minimum(g, SWIGLU_LIMIT)
            u = jnp.clip(u, -SWIGLU_LIMIT, SWIGLU_LIMIT)
            a = ((u + 1.0) * g * _sigmoid(SWIGLU_ALPHA * g)).astype(BF16)
            part = _dot(a, w_down_ref[c0:c0 + chunk, :])
            acc = part if acc is None else acc + part
        o_ref[...] = acc + b_down_ref[...]

    @pl.when(pl.program_id(0) >= n_tiles_ref[0])
    def _():
        o_ref[...] = jnp.zeros_like(o_ref)


def _moe_experts(x_sorted, tile_expert, n_tiles, w_gu, b_gu, w_down, b_down):
    n_rows, d = x_sorted.shape
    n_exp, _, d_gu = w_gu.shape
    d_expert = w_down.shape[1]
    max_tiles = n_rows // MOE_TILE

    def row_index(i, te, nt):
        return (jnp.minimum(i, nt[0] - 1), 0)

    def expert_index(i, te, nt):
        return (te[i], 0, 0)

    grid_spec = pltpu.PrefetchScalarGridSpec(
        num_scalar_prefetch=2,
        grid=(max_tiles,),
        in_specs=[pl.BlockSpec((MOE_TILE, d), row_index),
                  pl.BlockSpec((None, d, d_gu), expert_index),
                  pl.BlockSpec((None, 1, d_gu), expert_index),
                  pl.BlockSpec((None, d_expert, d), expert_index),
                  pl.BlockSpec((None, 1, d), expert_index)],
        out_specs=pl.BlockSpec((MOE_TILE, d), lambda i, te, nt: (i, 0)))
    return pl.pallas_call(
        functools.partial(_moe_body, chunk=min(512, d_expert)),
        grid_spec=grid_spec,
        out_shape=jax.ShapeDtypeStruct((n_rows, d), F32),
        compiler_params=_params(1),
        name="moe",
    )(tile_expert, n_tiles, x_sorted, w_gu, b_gu.reshape(n_exp, 1, d_gu), w_down, b_down.reshape(n_exp, 1, d))


def _rank_body(expert_ref, rank_ref, count_ref, carry_ref):
    i = pl.program_id(0)
    tm, lanes = expert_ref.shape

    @pl.when(i == 0)
    def _():
        carry_ref[...] = jnp.zeros_like(carry_ref)

    experts = expert_ref[...]
    lane = lax.broadcasted_iota(jnp.int32, (tm, lanes), 1)
    picks = [lane == experts[:, k:k + 1] for k in range(TOP_K)]
    chosen = functools.reduce(jnp.logical_or, picks)
    row = lax.broadcasted_iota(jnp.int32, (tm, tm), 0)
    col = lax.broadcasted_iota(jnp.int32, (tm, tm), 1)
    earlier = (col < row).astype(BF16)
    before = _dot(earlier, jnp.where(chosen, 1.0, 0.0).astype(BF16)) + carry_ref[...]
    rank = jnp.zeros((tm, lanes), F32)
    for k in range(TOP_K):
        rank = jnp.where(lane == k, jnp.sum(jnp.where(picks[k], before, 0.0), axis=-1, keepdims=True), rank)
    rank_ref[...] = rank.astype(jnp.int32)
    carry_ref[...] += jnp.sum(jnp.where(chosen, 1.0, 0.0), axis=0, keepdims=True)
    count_ref[...] = carry_ref[...].astype(jnp.int32)


def _rank(experts):
    n_tok, lanes = experts.shape
    tm = TOKEN_TILE
    return pl.pallas_call(
        _rank_body,
        grid=(n_tok // tm,),
        in_specs=[pl.BlockSpec((tm, lanes), lambda i: (i, 0))],
        out_specs=[pl.BlockSpec((tm, lanes), lambda i: (i, 0)), pl.BlockSpec((1, lanes), lambda i: (0, 0))],
        out_shape=[jax.ShapeDtypeStruct((n_tok, lanes), jnp.int32), jax.ShapeDtypeStruct((1, lanes), jnp.int32)],
        scratch_shapes=[pltpu.VMEM((1, lanes), F32)],
        compiler_params=_params(1),
        name="rank",
    )(experts)


def _layout(counts, n_assign):
    n_exp = counts.shape[0]
    padded = (counts + MOE_TILE - 1) // MOE_TILE * MOE_TILE
    pend = jnp.cumsum(padded)
    max_tiles = (n_assign + MOE_TILE - 1) // MOE_TILE + n_exp
    n_tiles = (pend[-1] // MOE_TILE).astype(jnp.int32)
    tile_id = jnp.minimum(jnp.arange(max_tiles, dtype=jnp.int32), n_tiles - 1)
    tile_expert = jnp.sum((pend[None, :] <= (tile_id * MOE_TILE)[:, None]).astype(jnp.int32), axis=1)
    return pend - padded, jnp.minimum(tile_expert, n_exp - 1), n_tiles.reshape(1), max_tiles * MOE_TILE


def _row_copy(src_ref, src_row, dst_ref, dst_row, sem):
    return pltpu.make_async_copy(src_ref.at[pl.ds(src_row, 1)], dst_ref.at[pl.ds(dst_row, 1)], sem)


def _dispatch_body(pos_ref, h_ref, init_ref, o_ref, sem):
    del init_ref
    tm = h_ref.shape[0]

    def issue(t, carry):
        for k in range(TOP_K):
            _row_copy(h_ref, t, o_ref, pos_ref[0, 0, t * TOP_K + k], sem).start()
        return carry

    lax.fori_loop(0, tm, issue, 0)

    for k in range(TOP_K):
        pltpu.make_async_copy(h_ref, o_ref.at[pl.ds(0, tm)], sem).wait()


def _dispatch(h, pos, buffer):
    n_tok, d = h.shape
    tm = min(TOKEN_TILE, n_tok)
    return pl.pallas_call(
        _dispatch_body,
        grid=(n_tok // tm,),
        in_specs=[pl.BlockSpec((1, 1, tm * TOP_K), lambda i: (i, 0, 0), memory_space=pltpu.SMEM),
                  pl.BlockSpec((tm, d), lambda i: (i, 0)),
                  pl.BlockSpec(memory_space=pl.ANY)],
        out_specs=pl.BlockSpec(memory_space=pl.ANY),
        out_shape=jax.ShapeDtypeStruct(buffer.shape, buffer.dtype),
        scratch_shapes=[pltpu.SemaphoreType.DMA(())],
        input_output_aliases={2: 0},
        compiler_params=_params(1),
        name="dispatch",
    )(pos.reshape(n_tok // tm, 1, tm * TOP_K), h, buffer)


def _combine_body(pos_ref, y_ref, gate_ref, x1_ref, gt2_ref, g_ref, o_ref, buf_ref, sem):
    tm = x1_ref.shape[0]

    def issue(t, carry):
        for k in range(TOP_K):
            _row_copy(y_ref, pos_ref[0, 0, t * TOP_K + k], buf_ref.at[k], t, sem).start()
        return carry

    lax.fori_loop(0, tm, issue, 0)

    for k in range(TOP_K):
        pltpu.make_async_copy(y_ref.at[pl.ds(0, tm)], buf_ref.at[k], sem).wait()
    gates = gate_ref[...]
    y = gates[:, 0:1] * buf_ref[0]
    for k in range(1, TOP_K):
        y += gates[:, k:k + 1] * buf_ref[k]
    o_ref[...] = x1_ref[...] + gt2_ref[...] * _rms(y, g_ref[...])


def _combine(y_sorted, pos, gates, x1, gt2, g, *, per_token, rows_per_seq):
    rows, d = x1.shape
    tm = min(TOKEN_TILE, rows)
    row_spec = lambda w: pl.BlockSpec((tm, w), lambda i: (i, 0))
    return pl.pallas_call(
        _combine_body,
        grid=(rows // tm,),
        in_specs=[pl.BlockSpec((1, 1, tm * TOP_K), lambda i: (i, 0, 0), memory_space=pltpu.SMEM),
                  pl.BlockSpec(memory_space=pl.ANY),
                  row_spec(gates.shape[1]), row_spec(d), _mod_spec(per_token, tm, d, rows_per_seq),
                  pl.BlockSpec((1, d), lambda i: (0, 0))],
        out_specs=row_spec(d),
        out_shape=jax.ShapeDtypeStruct((rows, d), F32),
        scratch_shapes=[pltpu.VMEM((TOP_K, tm, d), F32), pltpu.SemaphoreType.DMA(())],
        compiler_params=_params(1),
        name="combine",
    )(pos.reshape(rows // tm, 1, tm * TOP_K), y_sorted, gates, x1, gt2, g.reshape(1, d))


def kernel(x_prompt, x_sample, cache_k, cache_v, state_conv, page_table, c_prompt, c_sample, w_mod, b_mod, g_pre_mix, g_post_mix, w_in, b_sb, w_att_out, w_dw, b_dw, ln_conv_g, ln_conv_b, w_conv_out, b_conv_out, w_out, g_pre_ffn, g_post_ffn, w_router, b_router, w_gu, b_gu, w_down, b_down):
    depth = w_mod.shape[0]
    assert depth == 1, "single-layer trunk"
    n, s, d = x_prompt.shape
    nb, t_new, _ = x_sample.shape
    d_att = w_att_out.shape[1]
    d_conv = w_dw.shape[2]
    width = w_dw.shape[1]
    n_exp = w_router.shape[2]
    head_dim = d_att // N_HEADS
    page = cache_k.shape[2]
    assert width - 1 <= HALO and s % TOKEN_TILE == 0
    l = 0

    row2 = lambda a: a.reshape(1, -1)
    w_in_b = w_in[l].astype(BF16)
    w_kvt = w_in_b[:, d_att:3 * d_att].T
    merge_w = (w_att_out[l].astype(BF16), row2(ln_conv_g[l]), row2(ln_conv_b[l]),
               w_conv_out[l].astype(BF16), row2(b_conv_out[l]), w_out[l].astype(BF16),
               row2(g_post_mix[l]), row2(g_pre_ffn[l]), w_router[l].astype(BF16), row2(b_router[l]))

    n_seq = n + nb
    c_all = jnp.pad(jnp.concatenate([c_prompt, c_sample], axis=0), ((0, -n_seq % 8), (0, 0)))
    mod = _modulation(c_all, w_mod[l], b_mod[l])
    mod_p = [m.reshape(n, 1, d) for m in jnp.split(mod[:n], 6, axis=-1)]
    mod_s = jnp.split(jnp.repeat(mod[n:n_seq], t_new, axis=0), 6, axis=-1)

    def mixing(x, mods, attend, conv, per_token, rows_per_seq, kv_group):
        sh1, sc1, gt1, sh2, sc2, _ = mods
        q, kt, vt, kb, vb, u, ga, gc = _inproj(x, sc1, sh1, g_pre_mix[l], w_in_b, w_kvt, d_att=d_att, d_conv=d_conv,
                                               per_token=per_token, rows_per_seq=rows_per_seq, kv_group=kv_group)
        att = attend(q, kt, vt, kb, vb)
        dw = conv(u)
        x1, h2, gates, experts = _merge(att, dw, ga, gc, x, gt1, sc2, sh2, merge_w,
                                        per_token=per_token, rows_per_seq=rows_per_seq)
        return x1, h2, gates, experts, kt, vt, u

    def attend_prompt(q, kt, vt, kb, vb):
        shape = (n, s, d_att)
        return _sb_attention_prompt(q.reshape(shape), kb.reshape(shape), vb.reshape(shape),
                                    b_sb[l].astype(F32)).reshape(n * s, d_att)

    def conv_prompt(u):
        tm = min(512, s)
        dw = _depthwise_conv(u.reshape(n * s // HALO, HALO, d_conv), u.reshape(n, s, d_conv), w_dw[l], b_dw[l],
                             bs=1, tm=tm, tiles_per_seq=s // tm)
        return dw.reshape(n * s, d_conv)

    x1_p, h2_p, gates_p, experts_p, kt_p, vt_p, u_p = mixing(x_prompt.reshape(n * s, d), mod_p, attend_prompt,
                                                             conv_prompt, False, s, s)

    pool_k = cache_k[l].transpose(0, 2, 3, 1).reshape(-1, d_att, page)
    pool_v = cache_v[l].transpose(0, 2, 3, 1).reshape(-1, d_att, page)

    def attend_sample(q, kt, vt, kb, vb):
        per_seq = lambda a: a.reshape(d_att, nb, t_new).transpose(1, 0, 2)
        return _sb_attention_sample(q.reshape(nb, t_new, d_att), per_seq(kt), per_seq(vt), pool_k, pool_v,
                                    page_table, b_sb[l]).reshape(nb * t_new, d_att)

    def conv_sample(u):
        t_pad = -t_new % 8
        cur = jnp.pad(u.reshape(nb, t_new, d_conv), ((0, 0), (0, t_pad), (0, 0)))
        prev = jnp.pad(state_conv[l], ((0, 0), (HALO - (width - 1), 0), (0, 0)))
        dw = _depthwise_conv(prev, cur, w_dw[l], b_dw[l], bs=min(16, nb), tm=t_new + t_pad, tiles_per_seq=None)
        return dw[:, :t_new].reshape(nb * t_new, d_conv)

    x1_s, h2_s, gates_s, experts_s, kt_s, vt_s, u_s = mixing(x_sample.reshape(nb * t_new, d), mod_s, attend_sample,
                                                             conv_sample, True, t_new, nb * t_new)

    n_p = n * s
    n_tok = n_p + nb * t_new
    assert n_p % TOKEN_TILE == 0 and n_tok % TOKEN_TILE == 0
    experts = jnp.concatenate([experts_p, experts_s], axis=0)
    rank, counts = _rank(experts)
    row_start, tile_expert, n_tiles, n_rows = _layout(counts[0, :n_exp], n_tok * TOP_K)
    pos = row_start[experts[:, :TOP_K]] + rank[:, :TOP_K]
    x_sorted = _dispatch(h2_p, pos[:n_p], jnp.zeros((n_rows, d), F32))
    x_sorted = _dispatch(h2_s, pos[n_p:], x_sorted)
    y_sorted = _moe_experts(x_sorted, tile_expert, n_tiles, _to_bf16(w_gu[l]), b_gu[l], _to_bf16(w_down[l]), b_down[l])
    y_prompt = _combine(y_sorted, pos[:n_p], gates_p, x1_p, mod_p[5], g_post_ffn[l], per_token=False, rows_per_seq=s)
    y_sample = _combine(y_sorted, pos[n_p:], gates_s, x1_s, mod_s[5], g_post_ffn[l], per_token=True,
                        rows_per_seq=t_new)

    heads_last = lambda a, seqs, toks: a.reshape(N_HEADS, head_dim, seqs, toks).transpose(2, 3, 0, 1)[None]
    prompt_kv = lambda a: heads_last(a.transpose(1, 0, 2), n, s)
    sample_kv = lambda a: heads_last(a[0], nb, t_new)
    keep = width - 1
    conv_p = u_p.reshape(n, s, d_conv)[:, s - keep:]
    conv_s = jnp.concatenate([state_conv[l], u_s.reshape(nb, t_new, d_conv)], axis=1)[:, -keep:]
    return (y_prompt.reshape(n, s, d), y_sample.reshape(nb, t_new, d),
            prompt_kv(kt_p), prompt_kv(vt_p), conv_p[None],
            sample_kv(kt_s), sample_kv(vt_s), conv_s[None])
```

```python
import functools

import jax
import jax.numpy as jnp
from jax import lax
from jax.experimental import pallas as pl
from jax.experimental.pallas import tpu as pltpu

F32 = jnp.float32
BF16 = jnp.bfloat16

N_HEADS = 8
TOP_K = 4
SWIGLU_LIMIT = 7.0
SWIGLU_ALPHA = 1.702
RMS_EPS = 1e-6
LN_EPS = 1e-5
SOFTPLUS_LINEAR_ABOVE = 30.0

VMEM_LIMIT_BYTES = 48 * 1024 * 1024
HALO = 32
TOKEN_TILE = 256
ATT_KEY_TILE = 256
ATT_QUERY_TILE = 512
MOE_TILE = 512
LANES = 128
SUBLANES = 8
PAGES_PER_STEP = 8


def _params(n_axes):
    return pltpu.CompilerParams(dimension_semantics=("arbitrary",) * n_axes,
                                vmem_limit_bytes=VMEM_LIMIT_BYTES)


def _dot(a, b):
    return jnp.dot(a, b, preferred_element_type=F32)


def _dot_nt(a, b):
    return lax.dot_general(a, b, (((1,), (1,)), ((), ())), preferred_element_type=F32)


def _rms(x, g):
    return x * lax.rsqrt(jnp.mean(x * x, axis=-1, keepdims=True) + RMS_EPS) * g


def _sigmoid(x):
    return 1.0 / (1.0 + jnp.exp(-x))


def _mod_body(c_ref, w_ref, b_ref, o_ref):
    c = c_ref[...]
    s = (c * _sigmoid(c)).astype(BF16)
    o_ref[...] = _dot(s, w_ref[...].astype(BF16)) + b_ref[...]


def _modulation(c, w_mod, b_mod):
    n, d = c.shape
    n_out = w_mod.shape[1]
    tn = n_out // 6
    return pl.pallas_call(
        _mod_body,
        grid=(n_out // tn,),
        in_specs=[pl.BlockSpec((n, d), lambda j: (0, 0)),
                  pl.BlockSpec((d, tn), lambda j: (0, j)),
                  pl.BlockSpec((1, tn), lambda j: (0, j))],
        out_specs=pl.BlockSpec((n, tn), lambda j: (0, j)),
        out_shape=jax.ShapeDtypeStruct((n, n_out), F32),
        compiler_params=_params(1),
        name="mod",
    )(c, w_mod, b_mod.reshape(1, n_out))


def _inproj_body(x_ref, sc_ref, sh_ref, g_ref, w_ref, w_kvt_ref,
                 q_ref, kt_ref, vt_ref, kb_ref, vb_ref, u_ref, ga_ref, gc_ref, *, d_att, d_conv, q_scale):
    d_model = x_ref.shape[-1]
    h = _rms(x_ref[...], g_ref[...]) * (1.0 + sc_ref[...]) + sh_ref[...]
    hb = h.astype(BF16)

    def proj(lo, width):
        return _dot(hb, w_ref[:, lo:lo + width])

    q_ref[...] = (proj(0, d_att) * q_scale).astype(BF16)
    kb_ref[...] = proj(d_att, d_att).astype(BF16)
    vb_ref[...] = proj(2 * d_att, d_att).astype(BF16)
    kt_ref[...] = _dot_nt(w_kvt_ref[0:d_att, :], hb)
    vt_ref[...] = _dot_nt(w_kvt_ref[d_att:2 * d_att, :], hb)
    ca = proj(3 * d_att, d_conv)
    cb = proj(3 * d_att + d_conv, d_conv)
    u_ref[...] = ca * _sigmoid(cb)
    ga_ref[...] = _sigmoid(proj(3 * d_att + 2 * d_conv, d_model))
    gc_ref[...] = _sigmoid(proj(3 * d_att + 2 * d_conv + d_model, d_model))


def _mod_spec(per_token, tm, d, rows_per_seq):
    if per_token:
        return pl.BlockSpec((tm, d), lambda i: (i, 0))
    tiles_per_seq = rows_per_seq // tm
    return pl.BlockSpec((None, 1, d), lambda i: (i // tiles_per_seq, 0, 0))


def _inproj(x, sc, sh, g, w_in, w_kvt, *, d_att, d_conv, per_token, rows_per_seq, kv_group):
    rows, d = x.shape
    tm = min(TOKEN_TILE, rows)
    head_dim = d_att // N_HEADS
    tiles_per_group = kv_group // tm
    row_spec = lambda width: pl.BlockSpec((tm, width), lambda i: (i, 0))
    row_out = lambda width, dtype: (row_spec(width), jax.ShapeDtypeStruct((rows, width), dtype))
    kvt_out = (pl.BlockSpec((None, d_att, tm), lambda i: (i // tiles_per_group, 0, i % tiles_per_group)),
               jax.ShapeDtypeStruct((rows // kv_group, d_att, kv_group), F32))
    mod_spec = _mod_spec(per_token, tm, d, rows_per_seq)
    outs = (row_out(d_att, BF16), kvt_out, kvt_out, row_out(d_att, BF16), row_out(d_att, BF16),
            row_out(d_conv, F32), row_out(d, F32), row_out(d, F32))
    return pl.pallas_call(
        functools.partial(_inproj_body, d_att=d_att, d_conv=d_conv, q_scale=head_dim ** -0.5),
        grid=(rows // tm,),
        in_specs=[row_spec(d), mod_spec, mod_spec,
                  pl.BlockSpec((1, d), lambda i: (0, 0)),
                  pl.BlockSpec(w_in.shape, lambda i: (0, 0)),
                  pl.BlockSpec(w_kvt.shape, lambda i: (0, 0))],
        out_specs=[spec for spec, _ in outs],
        out_shape=[shape for _, shape in outs],
        compiler_params=_params(1),
        name="inproj",
    )(x, sc, sh, g.reshape(1, d), w_in, w_kvt)


def _suffix_matrix(tile):
    row = lax.broadcasted_iota(jnp.int32, (tile, tile), 0)
    col = lax.broadcasted_iota(jnp.int32, (tile, tile), 1)
    return (row >= col).astype(BF16)


def _sb_suffix(z, tri, valid):
    sp = jnp.where(z > SOFTPLUS_LINEAR_ABOVE, z, jnp.log(1.0 + jnp.exp(z)))
    if valid is not None:
        sp = jnp.where(valid, sp, 0.0)
    return _dot(sp.astype(BF16), tri)


def _sb_weights(z, right, valid):
    p = jnp.exp(z - right)
    if valid is not None:
        p = jnp.where(valid, p, 0.0)
    return p.astype(BF16)


def _sb_prompt_body(bias_ref, q_ref, k_ref, v_ref, o_ref,
                    qa_ref, z_ref, p_ref, acc_ref, ucur_ref, uprev_ref, *, tq, tk, head_dim):
    hp = pl.program_id(1)
    qi = pl.program_id(2)
    pair = 2 * head_dim
    ratio = tq // tk
    last = (qi + 1) * ratio - 1
    lane = lax.broadcasted_iota(jnp.int32, (1, pair), 1)
    row = lax.broadcasted_iota(jnp.int32, (2 * tq, tk), 0)
    col = lax.broadcasted_iota(jnp.int32, (2 * tq, tk), 1)
    q_pos = jnp.where(row >= tq, row - tq, row)
    causal = [col + (ratio - 1 - t) * tk < q_pos for t in range(ratio)]
    tri = _suffix_matrix(tk)
    ones = jnp.ones((tk, pair), BF16)

    q = q_ref[0]
    for h in range(2):
        qa_ref[h * tq:(h + 1) * tq, 0:pair] = jnp.where(
            (lane >= h * head_dim) & (lane < (h + 1) * head_dim), q, jnp.zeros_like(q))
        b = bias_ref[2 * hp + h]
        b1 = b.astype(BF16).astype(F32)
        b2 = (b - b1).astype(BF16).astype(F32)
        b3 = (b - b1 - b2).astype(BF16).astype(F32)
        offset = jnp.where(lane == 0, b1, jnp.where(lane == 1, b2, jnp.where(lane == 2, b3, 0.0)))
        qa_ref[h * tq:(h + 1) * tq, pair:2 * pair] = jnp.broadcast_to(offset, (tq, pair)).astype(BF16)

    def logits(t, slot):
        off = pl.multiple_of(jnp.maximum(last - t, 0) * tk, tk)
        ka = jnp.concatenate([k_ref[0, pl.ds(off, tk), :], ones], axis=1)
        z_ref[slot] = _dot_nt(qa_ref[...], ka)

    def weights(slot, valid):
        right = _sb_suffix(z_ref[slot], tri, valid)
        p_ref[slot] = _sb_weights(z_ref[slot], right, valid)
        return right[:, 0:1]

    def values(t, slot):
        off = pl.multiple_of((last - t) * tk, tk)
        acc_ref[...] += jnp.exp(uprev_ref[...]) * _dot(p_ref[slot], v_ref[0, pl.ds(off, tk), :])

    def stage(t, slot, valid):
        values(t - 1, 1 - slot)
        total = weights(slot, valid)
        logits(t + 1, 1 - slot)
        ucur = ucur_ref[...]
        uprev_ref[...] = ucur
        ucur_ref[...] = ucur - total

    logits(0, 0)
    logits(1, 1)
    ucur_ref[...] = -weights(0, causal[0])
    uprev_ref[...] = jnp.zeros_like(uprev_ref)
    acc_ref[...] = jnp.zeros_like(acc_ref)
    for t in range(1, ratio):
        stage(t, t % 2, causal[t])

    def two_stages(u, carry):
        stage(ratio + 2 * u, 0, None)
        stage(ratio + 2 * u + 1, 1, None)
        return carry

    lax.fori_loop(0, qi * (ratio // 2), two_stages, 0)
    values(last, 1)
    o_ref[0] = jnp.where(lane < head_dim, acc_ref[0:tq], acc_ref[tq:2 * tq]).astype(o_ref.dtype)


def _sb_attention_prompt(q, k, v, bias):
    n, s, d_att = q.shape
    head_dim = d_att // N_HEADS
    tk = ATT_KEY_TILE
    tq = ATT_QUERY_TILE
    assert s % tq == 0 and tq % (2 * tk) == 0
    pair = 2 * head_dim
    grid_spec = pltpu.PrefetchScalarGridSpec(
        num_scalar_prefetch=1,
        grid=(n, N_HEADS // 2, s // tq),
        in_specs=[pl.BlockSpec((1, tq, pair), lambda b, hp, qi, bias: (b, qi, hp)),
                  pl.BlockSpec((1, s, pair), lambda b, hp, qi, bias: (b, 0, hp)),
                  pl.BlockSpec((1, s, pair), lambda b, hp, qi, bias: (b, 0, hp))],
        out_specs=pl.BlockSpec((1, tq, pair), lambda b, hp, qi, bias: (b, qi, hp)),
        scratch_shapes=[pltpu.VMEM((2 * tq, 2 * pair), BF16),
                        pltpu.VMEM((2, 2 * tq, tk), F32),
                        pltpu.VMEM((2, 2 * tq, tk), BF16),
                        pltpu.VMEM((2 * tq, pair), F32),
                        pltpu.VMEM((2 * tq, 1), F32),
                        pltpu.VMEM((2 * tq, 1), F32)])
    return pl.pallas_call(
        functools.partial(_sb_prompt_body, tq=tq, tk=tk, head_dim=head_dim),
        grid_spec=grid_spec,
        out_shape=jax.ShapeDtypeStruct((n, s, d_att), BF16),
        compiler_params=_params(3),
        name="sb_prompt",
    )(bias, q, k, v)


def _sb_sample_body(pt_ref, qbd_ref, bias_ref, knew_ref, vnew_ref, *rest, page, t_new, n_steps):
    pages = rest[:2 * PAGES_PER_STEP]
    o_ref, kt_ref, vt_ref, acc_ref, used_ref = rest[2 * PAGES_PER_STEP:]
    g = pl.program_id(1)
    rows, d_att = acc_ref.shape
    head_dim = d_att // N_HEADS
    tri = _suffix_matrix(page)
    qbd = qbd_ref[0]
    bias = bias_ref[...]

    @pl.when(g == 0)
    def _():
        qidx = lax.broadcasted_iota(jnp.int32, (rows, page), 0) // N_HEADS
        kidx = lax.broadcasted_iota(jnp.int32, (rows, page), 1)
        valid = kidx < qidx
        z = _dot(qbd, knew_ref[0].astype(BF16)) + bias
        right = _sb_suffix(z, tri, valid)
        acc_ref[...] = _dot_nt(_sb_weights(z, right, valid), vnew_ref[0].astype(BF16))
        used_ref[...] = right[:, 0:1]

    for i in range(PAGES_PER_STEP):
        kt_ref[:, i * page:(i + 1) * page] = pages[2 * i][0].astype(BF16)
        vt_ref[:, i * page:(i + 1) * page] = pages[2 * i + 1][0].astype(BF16)
    z_wide = _dot(qbd, kt_ref[...])
    z = jnp.concatenate([z_wide[:, i * page:(i + 1) * page] + bias for i in range(PAGES_PER_STEP)], axis=0)
    right = _sb_suffix(z, tri, None)
    used = used_ref[...]
    used_rows = []
    for i in range(PAGES_PER_STEP):
        used_rows.append(used)
        used = used + right[i * rows:(i + 1) * rows, 0:1]
    used_ref[...] = used
    p = _sb_weights(z, right + jnp.concatenate(used_rows, axis=0), None)
    p_wide = jnp.concatenate([p[i * rows:(i + 1) * rows] for i in range(PAGES_PER_STEP)], axis=1)
    acc_ref[...] += _dot_nt(p_wide, vt_ref[...])

    @pl.when(g == n_steps - 1)
    def _():
        head = lax.broadcasted_iota(jnp.int32, (rows, d_att), 0) % N_HEADS
        lane_head = lax.broadcasted_iota(jnp.int32, (rows, d_att), 1) // head_dim
        own = jnp.where(head == lane_head, acc_ref[...], 0.0)
        o_ref[0] = jnp.sum(own.reshape(t_new, N_HEADS, d_att), axis=1)


def _sb_attention_sample(q, kt_new, vt_new, cache_k, cache_v, page_table, bias):
    nb, t_new, d_att = q.shape
    head_dim = d_att // N_HEADS
    page = cache_k.shape[2]
    n_pages = page_table.shape[1]
    n_steps = n_pages // PAGES_PER_STEP
    rows = t_new * N_HEADS
    head_mask = (jnp.arange(d_att)[None, :] // head_dim == jnp.arange(N_HEADS)[:, None]).astype(q.dtype)
    qbd = (q[:, :, None, :] * head_mask[None, None]).reshape(nb, rows, d_att)
    bias_rows = jnp.broadcast_to(jnp.tile(bias.astype(F32), t_new)[:, None], (rows, page))
    new_keys = lambda a: jnp.pad(a, ((0, 0), (0, 0), (0, page - t_new)))

    def page_spec(i):
        def index(b, g, pt):
            return (pt[b, n_pages - 1 - (g * PAGES_PER_STEP + i)], 0, 0)
        return pl.BlockSpec((1, d_att, page), index)

    page_specs, page_args = [], []
    for i in range(PAGES_PER_STEP):
        page_specs += [page_spec(i), page_spec(i)]
        page_args += [cache_k, cache_v]
    seq_spec = lambda r, c: pl.BlockSpec((1, r, c), lambda b, g, pt: (b, 0, 0))
    grid_spec = pltpu.PrefetchScalarGridSpec(
        num_scalar_prefetch=1,
        grid=(nb, n_steps),
        in_specs=[seq_spec(rows, d_att), pl.BlockSpec((rows, page), lambda b, g, pt: (0, 0)),
                  seq_spec(d_att, page), seq_spec(d_att, page)] + page_specs,
        out_specs=seq_spec(t_new, d_att),
        scratch_shapes=[pltpu.VMEM((d_att, PAGES_PER_STEP * page), BF16),
                        pltpu.VMEM((d_att, PAGES_PER_STEP * page), BF16),
                        pltpu.VMEM((rows, d_att), F32), pltpu.VMEM((rows, 1), F32)])
    return pl.pallas_call(
        functools.partial(_sb_sample_body, page=page, t_new=t_new, n_steps=n_steps),
        grid_spec=grid_spec,
        out_shape=jax.ShapeDtypeStruct((nb, t_new, d_att), F32),
        compiler_params=_params(2),
        name="sb_sample",
    )(page_table, qbd, bias_rows, new_keys(kt_new), new_keys(vt_new), *page_args)


def _conv_body(prev_ref, cur_ref, w_ref, b_ref, o_ref, ext_ref, *, width, tiles_per_seq, chunk):
    bs, tm, _ = cur_ref.shape
    span = ext_ref.shape[2]
    prev = prev_ref[...]
    if tiles_per_seq is not None:
        prev = jnp.where(pl.program_id(0) % tiles_per_seq == 0, 0.0, prev)
    ext_ref[0, :, 0:HALO, :] = prev
    ext_ref[0, :, HALO:HALO + tm, :] = cur_ref[...]
    for s in range(1, SUBLANES):
        ext_ref[s, :, 0:span - SUBLANES, :] = ext_ref[0, :, s:s + span - SUBLANES, :]
    first = HALO - (width - 1)
    for b in range(bs):
        for r0 in range(0, tm, chunk):
            acc = None
            for w in range(width):
                shift = (first + w) % SUBLANES
                base = first + w - shift + r0
                term = ext_ref[shift, b, base:base + chunk, :] * w_ref[w:w + 1, :]
                acc = term if acc is None else acc + term
            o_ref[b, r0:r0 + chunk, :] = acc + b_ref[...]


def _depthwise_conv(prev, cur, w_dw, b_dw, *, bs, tm, tiles_per_seq):
    n, rows, c = cur.shape
    width = w_dw.shape[0]
    w_pad = jnp.pad(w_dw, ((0, HALO - width), (0, 0)))
    if tiles_per_seq is None:
        grid = (n // bs,)
        prev_index = lambda i: (i, 0, 0)
        cur_index = lambda i: (i, 0, 0)
    else:
        grid = (n * tiles_per_seq,)
        halo_per_tile = tm // HALO
        prev_index = lambda i: (jnp.maximum(i * halo_per_tile - 1, 0), 0, 0)
        cur_index = lambda i: (i // tiles_per_seq, i % tiles_per_seq, 0)
    return pl.pallas_call(
        functools.partial(_conv_body, width=width, tiles_per_seq=tiles_per_seq, chunk=min(64, tm)),
        grid=grid,
        in_specs=[pl.BlockSpec((bs, HALO, c), prev_index),
                  pl.BlockSpec((bs, tm, c), cur_index),
                  pl.BlockSpec((HALO, c), lambda i: (0, 0)),
                  pl.BlockSpec((1, c), lambda i: (0, 0))],
        out_specs=pl.BlockSpec((bs, tm, c), cur_index),
        out_shape=jax.ShapeDtypeStruct((n, rows, c), F32),
        scratch_shapes=[pltpu.VMEM((SUBLANES, bs, HALO + tm, c), F32)],
        compiler_params=_params(1),
        name="conv",
    )(prev, cur, w_pad, b_dw.reshape(1, c))


def _merge_body(att_ref, dw_ref, ga_ref, gc_ref, x_ref, gt1_ref, sc2_ref, sh2_ref,
                w_att_ref, ln_g_ref, ln_b_ref, w_conv_ref, b_conv_ref, w_out_ref,
                g_post_ref, g_pre_ref, w_router_ref, b_router_ref,
                x1_ref, h2_ref, gate_ref, expert_ref):
    y_att = _dot(att_ref[...].astype(BF16), w_att_ref[...])
    dw = dw_ref[...]
    mu = jnp.mean(dw, axis=-1, keepdims=True)
    xc = dw - mu
    ln = xc * lax.rsqrt(jnp.mean(xc * xc, axis=-1, keepdims=True) + LN_EPS) * ln_g_ref[...] + ln_b_ref[...]
    act = ln * _sigmoid(ln)
    y_conv = _dot(act.astype(BF16), w_conv_ref[...]) + b_conv_ref[...]
    mixed = ga_ref[...] * y_att + gc_ref[...] * y_conv
    out = _dot(mixed.astype(BF16), w_out_ref[...])
    x1 = x_ref[...] + gt1_ref[...] * _rms(out, g_post_ref[...])
    x1_ref[...] = x1
    h2 = _rms(x1, g_pre_ref[...]) * (1.0 + sc2_ref[...]) + sh2_ref[...]
    h2_ref[...] = h2
    logits = _dot(h2.astype(BF16), w_router_ref[...]) + b_router_ref[...]

    n_exp = logits.shape[1]
    lane = lax.broadcasted_iota(jnp.int32, logits.shape, 1)
    out_lane = lax.broadcasted_iota(jnp.int32, gate_ref.shape, 1)
    experts = jnp.zeros(expert_ref.shape, jnp.int32)
    weights = jnp.zeros(gate_ref.shape, F32)
    top = None
    for k in range(TOP_K):
        best = jnp.max(logits, axis=-1, keepdims=True)
        idx = jnp.min(jnp.where(logits == best, lane, n_exp), axis=-1, keepdims=True)
        top = best if top is None else top
        experts = jnp.where(out_lane == k, idx, experts)
        weights = jnp.where(out_lane == k, jnp.exp(best - top), weights)
        logits = jnp.where(lane == idx, -jnp.inf, logits)
    gate_ref[...] = weights / jnp.sum(weights, axis=-1, keepdims=True)
    expert_ref[...] = experts


def _merge(att, dw, ga, gc, x, gt1, sc2, sh2, weights, *, per_token, rows_per_seq):
    rows, d = x.shape
    tm = min(TOKEN_TILE, rows)
    row_spec = lambda a: pl.BlockSpec((tm, a.shape[1]), lambda i: (i, 0))
    mod_spec = _mod_spec(per_token, tm, d, rows_per_seq)
    full_spec = lambda a: pl.BlockSpec(a.shape, lambda i: (0, 0))
    out_widths = (d, d, LANES, LANES)
    out_dtypes = (F32, F32, F32, jnp.int32)
    return pl.pallas_call(
        _merge_body,
        grid=(rows // tm,),
        in_specs=[row_spec(a) for a in (att, dw, ga, gc, x)] + [mod_spec] * 3 + [full_spec(w) for w in weights],
        out_specs=[pl.BlockSpec((tm, w), lambda i: (i, 0)) for w in out_widths],
        out_shape=[jax.ShapeDtypeStruct((rows, w), t) for w, t in zip(out_widths, out_dtypes)],
        compiler_params=_params(1),
        name="merge",
    )(att, dw, ga, gc, x, gt1, sc2, sh2, *weights)


def _to_bf16_body(w_ref, o_ref):
    o_ref[...] = w_ref[...].astype(BF16)


def _to_bf16(w):
    n_exp, rows, cols = w.shape
    tr = min(512, rows)
    spec = pl.BlockSpec((None, tr, cols), lambda e, r: (e, r, 0))
    return pl.pallas_call(
        _to_bf16_body,
        grid=(n_exp, rows // tr),
        in_specs=[spec],
        out_specs=spec,
        out_shape=jax.ShapeDtypeStruct(w.shape, BF16),
        compiler_params=_params(2),
        name="to_bf16",
    )(w)


def _moe_body(tile_expert_ref, n_tiles_ref, x_ref, w_gu_ref, b_gu_ref, w_down_ref, b_down_ref, o_ref, *, chunk):
    d_expert = w_down_ref.shape[0]

    @pl.when(pl.program_id(0) < n_tiles_ref[0])
    def _():
        x = x_ref[...].astype(BF16)
        acc = None
        for c0 in range(0, d_expert, chunk):
            g = _dot(x, w_gu_ref[:, c0:c0 + chunk]) + b_gu_ref[:, c0:c0 + chunk]
            u = _dot(x, w_gu_ref[:, d_expert + c0:d_expert + c0 + chunk]) + b_gu_ref[:, d_expert + c0:d_expert + c0 + chunk]
            g = jnp.minimum(g, SWIGLU_LIMIT)
            u = jnp.clip(u, -SWIGLU_LIMIT, SWIGLU_LIMIT)
            a = ((u + 1.0) * g * _sigmoid(SWIGLU_ALPHA * g)).astype(BF16)
            part = _dot(a, w_down_ref[c0:c0 + chunk, :])
            acc = part if acc is None else acc + part
        o_ref[...] = acc + b_down_ref[...]

    @pl.when(pl.program_id(0) >= n_tiles_ref[0])
    def _():
        o_ref[...] = jnp.zeros_like(o_ref)


def _moe_experts(x_sorted, tile_expert, n_tiles, w_gu, b_gu, w_down, b_down):
    n_rows, d = x_sorted.shape
    n_exp, _, d_gu = w_gu.shape
    d_expert = w_down.shape[1]
    max_tiles = n_rows // MOE_TILE

    def row_index(i, te, nt):
        return (jnp.minimum(i, nt[0] - 1), 0)

    def expert_index(i, te, nt):
        return (te[i], 0, 0)

    grid_spec = pltpu.PrefetchScalarGridSpec(
        num_scalar_prefetch=2,
        grid=(max_tiles,),
        in_specs=[pl.BlockSpec((MOE_TILE, d), row_index),
                  pl.BlockSpec((None, d, d_gu), expert_index),
                  pl.BlockSpec((None, 1, d_gu), expert_index),
                  pl.BlockSpec((None, d_expert, d), expert_index),
                  pl.BlockSpec((None, 1, d), expert_index)],
        out_specs=pl.BlockSpec((MOE_TILE, d), lambda i, te, nt: (i, 0)))
    return pl.pallas_call(
        functools.partial(_moe_body, chunk=min(512, d_expert)),
        grid_spec=grid_spec,
        out_shape=jax.ShapeDtypeStruct((n_rows, d), F32),
        compiler_params=_params(1),
        name="moe",
    )(tile_expert, n_tiles, x_sorted, w_gu, b_gu.reshape(n_exp, 1, d_gu), w_down, b_down.reshape(n_exp, 1, d))


def _rank_body(expert_ref, rank_ref, count_ref, carry_ref):
    i = pl.program_id(0)
    tm, lanes = expert_ref.shape

    @pl.when(i == 0)
    def _():
        carry_ref[...] = jnp.zeros_like(carry_ref)

    experts = expert_ref[...]
    lane = lax.broadcasted_iota(jnp.int32, (tm, lanes), 1)
    picks = [lane == experts[:, k:k + 1] for k in range(TOP_K)]
    chosen = functools.reduce(jnp.logical_or, picks)
    row = lax.broadcasted_iota(jnp.int32, (tm, tm), 0)
    col = lax.broadcasted_iota(jnp.int32, (tm, tm), 1)
    earlier = (col < row).astype(BF16)
    before = _dot(earlier, jnp.where(chosen, 1.0, 0.0).astype(BF16)) + carry_ref[...]
    rank = jnp.zeros((tm, lanes), F32)
    for k in range(TOP_K):
        rank = jnp.where(lane == k, jnp.sum(jnp.where(picks[k], before, 0.0), axis=-1, keepdims=True), rank)
    rank_ref[...] = rank.astype(jnp.int32)
    carry_ref[...] += jnp.sum(jnp.where(chosen, 1.0, 0.0), axis=0, keepdims=True)
    count_ref[...] = carry_ref[...].astype(jnp.int32)


def _rank(experts):
    n_tok, lanes = experts.shape
    tm = TOKEN_TILE
    return pl.pallas_call(
        _rank_body,
        grid=(n_tok // tm,),
        in_specs=[pl.BlockSpec((tm, lanes), lambda i: (i, 0))],
        out_specs=[pl.BlockSpec((tm, lanes), lambda i: (i, 0)), pl.BlockSpec((1, lanes), lambda i: (0, 0))],
        out_shape=[jax.ShapeDtypeStruct((n_tok, lanes), jnp.int32), jax.ShapeDtypeStruct((1, lanes), jnp.int32)],
        scratch_shapes=[pltpu.VMEM((1, lanes), F32)],
        compiler_params=_params(1),
        name="rank",
    )(experts)


def _layout(counts, n_assign):
    n_exp = counts.shape[0]
    padded = (counts + MOE_TILE - 1) // MOE_TILE * MOE_TILE
    pend = jnp.cumsum(padded)
    max_tiles = (n_assign + MOE_TILE - 1) // MOE_TILE + n_exp
    n_tiles = (pend[-1] // MOE_TILE).astype(jnp.int32)
    tile_id = jnp.minimum(jnp.arange(max_tiles, dtype=jnp.int32), n_tiles - 1)
    tile_expert = jnp.sum((pend[None, :] <= (tile_id * MOE_TILE)[:, None]).astype(jnp.int32), axis=1)
    return pend - padded, jnp.minimum(tile_expert, n_exp - 1), n_tiles.reshape(1), max_tiles * MOE_TILE


def _row_copy(src_ref, src_row, dst_ref, dst_row, sem):
    return pltpu.make_async_copy(src_ref.at[pl.ds(src_row, 1)], dst_ref.at[pl.ds(dst_row, 1)], sem)


def _dispatch_body(pos_ref, h_ref, init_ref, o_ref, sem):
    del init_ref
    tm = h_ref.shape[0]

    def issue(t, carry):
        for k in range(TOP_K):
            _row_copy(h_ref, t, o_ref, pos_ref[0, 0, t * TOP_K + k], sem).start()
        return carry

    lax.fori_loop(0, tm, issue, 0)

    for k in range(TOP_K):
        pltpu.make_async_copy(h_ref, o_ref.at[pl.ds(0, tm)], sem).wait()


def _dispatch(h, pos, buffer):
    n_tok, d = h.shape
    tm = min(TOKEN_TILE, n_tok)
    return pl.pallas_call(
        _dispatch_body,
        grid=(n_tok // tm,),
        in_specs=[pl.BlockSpec((1, 1, tm * TOP_K), lambda i: (i, 0, 0), memory_space=pltpu.SMEM),
                  pl.BlockSpec((tm, d), lambda i: (i, 0)),
                  pl.BlockSpec(memory_space=pl.ANY)],
        out_specs=pl.BlockSpec(memory_space=pl.ANY),
        out_shape=jax.ShapeDtypeStruct(buffer.shape, buffer.dtype),
        scratch_shapes=[pltpu.SemaphoreType.DMA(())],
        input_output_aliases={2: 0},
        compiler_params=_params(1),
        name="dispatch",
    )(pos.reshape(n_tok // tm, 1, tm * TOP_K), h, buffer)


def _combine_body(pos_ref, y_ref, gate_ref, x1_ref, gt2_ref, g_ref, o_ref, buf_ref, sem):
    tm = x1_ref.shape[0]

    def issue(t, carry):
        for k in range(TOP_K):
            _row_copy(y_ref, pos_ref[0, 0, t * TOP_K + k], buf_ref.at[k], t, sem).start()
        return carry

    lax.fori_loop(0, tm, issue, 0)

    for k in range(TOP_K):
        pltpu.make_async_copy(y_ref.at[pl.ds(0, tm)], buf_ref.at[k], sem).wait()
    gates = gate_ref[...]
    y = gates[:, 0:1] * buf_ref[0]
    for k in range(1, TOP_K):
        y += gates[:, k:k + 1] * buf_ref[k]
    o_ref[...] = x1_ref[...] + gt2_ref[...] * _rms(y, g_ref[...])


def _combine(y_sorted, pos, gates, x1, gt2, g, *, per_token, rows_per_seq):
    rows, d = x1.shape
    tm = min(TOKEN_TILE, rows)
    row_spec = lambda w: pl.BlockSpec((tm, w), lambda i: (i, 0))
    return pl.pallas_call(
        _combine_body,
        grid=(rows // tm,),
        in_specs=[pl.BlockSpec((1, 1, tm * TOP_K), lambda i: (i, 0, 0), memory_space=pltpu.SMEM),
                  pl.BlockSpec(memory_space=pl.ANY),
                  row_spec(gates.shape[1]), row_spec(d), _mod_spec(per_token, tm, d, rows_per_seq),
                  pl.BlockSpec((1, d), lambda i: (0, 0))],
        out_specs=row_spec(d),
        out_shape=jax.ShapeDtypeStruct((rows, d), F32),
        scratch_shapes=[pltpu.VMEM((TOP_K, tm, d), F32), pltpu.SemaphoreType.DMA(())],
        compiler_params=_params(1),
        name="combine",
    )(pos.reshape(rows // tm, 1, tm * TOP_K), y_sorted, gates, x1, gt2, g.reshape(1, d))


def kernel(x_prompt, x_sample, cache_k, cache_v, state_conv, page_table, c_prompt, c_sample, w_mod, b_mod, g_pre_mix, g_post_mix, w_in, b_sb, w_att_out, w_dw, b_dw, ln_conv_g, ln_conv_b, w_conv_out, b_conv_out, w_out, g_pre_ffn, g_post_ffn, w_router, b_router, w_gu, b_gu, w_down, b_down):
    depth = w_mod.shape[0]
    assert depth == 1, "single-layer trunk"
    n, s, d = x_prompt.shape
    nb, t_new, _ = x_sample.shape
    d_att = w_att_out.shape[1]
    d_conv = w_dw.shape[2]
    width = w_dw.shape[1]
    n_exp = w_router.shape[2]
    head_dim = d_att // N_HEADS
    page = cache_k.shape[2]
    assert width - 1 <= HALO and s % TOKEN_TILE == 0
    l = 0

    row2 = lambda a: a.reshape(1, -1)
    w_in_b = w_in[l].astype(BF16)
    w_kvt = w_in_b[:, d_att:3 * d_att].T
    merge_w = (w_att_out[l].astype(BF16), row2(ln_conv_g[l]), row2(ln_conv_b[l]),
               w_conv_out[l].astype(BF16), row2(b_conv_out[l]), w_out[l].astype(BF16),
               row2(g_post_mix[l]), row2(g_pre_ffn[l]), w_router[l].astype(BF16), row2(b_router[l]))

    n_seq = n + nb
    c_all = jnp.pad(jnp.concatenate([c_prompt, c_sample], axis=0), ((0, -n_seq % 8), (0, 0)))
    mod = _modulation(c_all, w_mod[l], b_mod[l])
    mod_p = [m.reshape(n, 1, d) for m in jnp.split(mod[:n], 6, axis=-1)]
    mod_s = jnp.split(jnp.repeat(mod[n:n_seq], t_new, axis=0), 6, axis=-1)

    def mixing(x, mods, attend, conv, per_token, rows_per_seq, kv_group):
        sh1, sc1, gt1, sh2, sc2, _ = mods
        q, kt, vt, kb, vb, u, ga, gc = _inproj(x, sc1, sh1, g_pre_mix[l], w_in_b, w_kvt, d_att=d_att, d_conv=d_conv,
                                               per_token=per_token, rows_per_seq=rows_per_seq, kv_group=kv_group)
        att = attend(q, kt, vt, kb, vb)
        dw = conv(u)
        x1, h2, gates, experts = _merge(att, dw, ga, gc, x, gt1, sc2, sh2, merge_w,
                                        per_token=per_token, rows_per_seq=rows_per_seq)
        return x1, h2, gates, experts, kt, vt, u

    def attend_prompt(q, kt, vt, kb, vb):
        shape = (n, s, d_att)
        return _sb_attention_prompt(q.reshape(shape), kb.reshape(shape), vb.reshape(shape),
                                    b_sb[l].astype(F32)).reshape(n * s, d_att)

    def conv_prompt(u):
        tm = min(512, s)
        dw = _depthwise_conv(u.reshape(n * s // HALO, HALO, d_conv), u.reshape(n, s, d_conv), w_dw[l], b_dw[l],
                             bs=1, tm=tm, tiles_per_seq=s // tm)
        return dw.reshape(n * s, d_conv)

    x1_p, h2_p, gates_p, experts_p, kt_p, vt_p, u_p = mixing(x_prompt.reshape(n * s, d), mod_p, attend_prompt,
                                                             conv_prompt, False, s, s)

    pool_k = cache_k[l].transpose(0, 2, 3, 1).reshape(-1, d_att, page)
    pool_v = cache_v[l].transpose(0, 2, 3, 1).reshape(-1, d_att, page)

    def attend_sample(q, kt, vt, kb, vb):
        per_seq = lambda a: a.reshape(d_att, nb, t_new).transpose(1, 0, 2)
        return _sb_attention_sample(q.reshape(nb, t_new, d_att), per_seq(kt), per_seq(vt), pool_k, pool_v,
                                    page_table, b_sb[l]).reshape(nb * t_new, d_att)

    def conv_sample(u):
        t_pad = -t_new % 8
        cur = jnp.pad(u.reshape(nb, t_new, d_conv), ((0, 0), (0, t_pad), (0, 0)))
        prev = jnp.pad(state_conv[l], ((0, 0), (HALO - (width - 1), 0), (0, 0)))
        dw = _depthwise_conv(prev, cur, w_dw[l], b_dw[l], bs=min(16, nb), tm=t_new + t_pad, tiles_per_seq=None)
        return dw[:, :t_new].reshape(nb * t_new, d_conv)

    x1_s, h2_s, gates_s, experts_s, kt_s, vt_s, u_s = mixing(x_sample.reshape(nb * t_new, d), mod_s, attend_sample,
                                                             conv_sample, True, t_new, nb * t_new)

    n_p = n * s
    n_tok = n_p + nb * t_new
    assert n_p % TOKEN_TILE == 0 and n_tok % TOKEN_TILE == 0
    experts = jnp.concatenate([experts_p, experts_s], axis=0)
    rank, counts = _rank(experts)
    row_start, tile_expert, n_tiles, n_rows = _layout(counts[0, :n_exp], n_tok * TOP_K)
    pos = row_start[experts[:, :TOP_K]] + rank[:, :TOP_K]
    x_sorted = _dispatch(h2_p, pos[:n_p], jnp.zeros((n_rows, d), F32))
    x_sorted = _dispatch(h2_s, pos[n_p:], x_sorted)
    y_sorted = _moe_experts(x_sorted, tile_expert, n_tiles, _to_bf16(w_gu[l]), b_gu[l], _to_bf16(w_down[l]), b_down[l])
    y_prompt = _combine(y_sorted, pos[:n_p], gates_p, x1_p, mod_p[5], g_post_ffn[l], per_token=False, rows_per_seq=s)
    y_sample = _combine(y_sorted, pos[n_p:], gates_s, x1_s, mod_s[5], g_post_ffn[l], per_token=True,
                        rows_per_seq=t_new)

    heads_last = lambda a, seqs, toks: a.reshape(N_HEADS, head_dim, seqs, toks).transpose(2, 3, 0, 1)[None]
    prompt_kv = lambda a: heads_last(a.transpose(1, 0, 2), n, s)
    sample_kv = lambda a: heads_last(a[0], nb, t_new)
    keep = width - 1
    conv_p = u_p.reshape(n, s, d_conv)[:, s - keep:]
    conv_s = jnp.concatenate([state_conv[l], u_s.reshape(nb, t_new, d_conv)], axis=1)[:, -keep:]
    return (y_prompt.reshape(n, s, d), y_sample.reshape(nb, t_new, d),
            prompt_kv(kt_p), prompt_kv(vt_p), conv_p[None],
            sample_kv(kt_s), sample_kv(vt_s), conv_s[None])
```

```python
import functools

import jax
import jax.numpy as jnp
from jax import lax
from jax.experimental import pallas as pl
from jax.experimental.pallas import tpu as pltpu

F32 = jnp.float32
BF16 = jnp.bfloat16

N_HEADS = 8
TOP_K = 4
SWIGLU_LIMIT = 7.0
SWIGLU_ALPHA = 1.702
RMS_EPS = 1e-6
LN_EPS = 1e-5
SOFTPLUS_LINEAR_ABOVE = 30.0

VMEM_LIMIT_BYTES = 48 * 1024 * 1024
HALO = 32
TOKEN_TILE = 256
ATT_KEY_TILE = 256
ATT_QUERY_TILE = 512
MOE_TILE = 512
LANES = 128
SUBLANES = 8
PAGES_PER_STEP = 8


def _params(n_axes):
    return pltpu.CompilerParams(dimension_semantics=("arbitrary",) * n_axes,
                                vmem_limit_bytes=VMEM_LIMIT_BYTES)


def _dot(a, b):
    return jnp.dot(a, b, preferred_element_type=F32)


def _dot_nt(a, b):
    return lax.dot_general(a, b, (((1,), (1,)), ((), ())), preferred_element_type=F32)


def _rms(x, g):
    return x * lax.rsqrt(jnp.mean(x * x, axis=-1, keepdims=True) + RMS_EPS) * g


def _sigmoid(x):
    return 1.0 / (1.0 + jnp.exp(-x))


def _mod_body(c_ref, w_ref, b_ref, o_ref):
    c = c_ref[...]
    s = (c * _sigmoid(c)).astype(BF16)
    o_ref[...] = _dot(s, w_ref[...].astype(BF16)) + b_ref[...]


def _modulation(c, w_mod, b_mod):
    n, d = c.shape
    n_out = w_mod.shape[1]
    tn = n_out // 6
    return pl.pallas_call(
        _mod_body,
        grid=(n_out // tn,),
        in_specs=[pl.BlockSpec((n, d), lambda j: (0, 0)),
                  pl.BlockSpec((d, tn), lambda j: (0, j)),
                  pl.BlockSpec((1, tn), lambda j: (0, j))],
        out_specs=pl.BlockSpec((n, tn), lambda j: (0, j)),
        out_shape=jax.ShapeDtypeStruct((n, n_out), F32),
        compiler_params=_params(1),
        name="mod",
    )(c, w_mod, b_mod.reshape(1, n_out))


def _inproj_body(x_ref, sc_ref, sh_ref, g_ref, w_ref, w_kvt_ref,
                 q_ref, kt_ref, vt_ref, kb_ref, vb_ref, u_ref, ga_ref, gc_ref, *, d_att, d_conv, q_scale):
    d_model = x_ref.shape[-1]
    h = _rms(x_ref[...], g_ref[...]) * (1.0 + sc_ref[...]) + sh_ref[...]
    hb = h.astype(BF16)

    def proj(lo, width):
        return _dot(hb, w_ref[:, lo:lo + width])

    q_ref[...] = (proj(0, d_att) * q_scale).astype(BF16)
    kb_ref[...] = proj(d_att, d_att).astype(BF16)
    vb_ref[...] = proj(2 * d_att, d_att).astype(BF16)
    kt_ref[...] = _dot_nt(w_kvt_ref[0:d_att, :], hb)
    vt_ref[...] = _dot_nt(w_kvt_ref[d_att:2 * d_att, :], hb)
    ca = proj(3 * d_att, d_conv)
    cb = proj(3 * d_att + d_conv, d_conv)
    u_ref[...] = ca * _sigmoid(cb)
    ga_ref[...] = _sigmoid(proj(3 * d_att + 2 * d_conv, d_model))
    gc_ref[...] = _sigmoid(proj(3 * d_att + 2 * d_conv + d_model, d_model))


def _mod_spec(per_token, tm, d, rows_per_seq):
    if per_token:
        return pl.BlockSpec((tm, d), lambda i: (i, 0))
    tiles_per_seq = rows_per_seq // tm
    return pl.BlockSpec((None, 1, d), lambda i: (i // tiles_per_seq, 0, 0))


def _inproj(x, sc, sh, g, w_in, w_kvt, *, d_att, d_conv, per_token, rows_per_seq, kv_group):
    rows, d = x.shape
    tm = min(TOKEN_TILE, rows)
    head_dim = d_att // N_HEADS
    tiles_per_group = kv_group // tm
    row_spec = lambda width: pl.BlockSpec((tm, width), lambda i: (i, 0))
    row_out = lambda width, dtype: (row_spec(width), jax.ShapeDtypeStruct((rows, width), dtype))
    kvt_out = (pl.BlockSpec((None, d_att, tm), lambda i: (i // tiles_per_group, 0, i % tiles_per_group)),
               jax.ShapeDtypeStruct((rows // kv_group, d_att, kv_group), F32))
    mod_spec = _mod_spec(per_token, tm, d, rows_per_seq)
    outs = (row_out(d_att, BF16), kvt_out, kvt_out, row_out(d_att, BF16), row_out(d_att, BF16),
            row_out(d_conv, F32), row_out(d, F32), row_out(d, F32))
    return pl.pallas_call(
        functools.partial(_inproj_body, d_att=d_att, d_conv=d_conv, q_scale=head_dim ** -0.5),
        grid=(rows // tm,),
        in_specs=[row_spec(d), mod_spec, mod_spec,
                  pl.BlockSpec((1, d), lambda i: (0, 0)),
                  pl.BlockSpec(w_in.shape, lambda i: (0, 0)),
                  pl.BlockSpec(w_kvt.shape, lambda i: (0, 0))],
        out_specs=[spec for spec, _ in outs],
        out_shape=[shape for _, shape in outs],
        compiler_params=_params(1),
        name="inproj",
    )(x, sc, sh, g.reshape(1, d), w_in, w_kvt)


def _suffix_matrix(tile):
    row = lax.broadcasted_iota(jnp.int32, (tile, tile), 0)
    col = lax.broadcasted_iota(jnp.int32, (tile, tile), 1)
    return (row >= col).astype(BF16)


def _sb_suffix(z, tri, valid):
    sp = jnp.where(z > SOFTPLUS_LINEAR_ABOVE, z, jnp.log(1.0 + jnp.exp(z)))
    if valid is not None:
        sp = jnp.where(valid, sp, 0.0)
    return _dot(sp.astype(BF16), tri)


def _sb_weights(z, right, valid):
    p = jnp.exp(z - right)
    if valid is not None:
        p = jnp.where(valid, p, 0.0)
    return p.astype(BF16)


def _sb_prompt_body(bias_ref, q_ref, k_ref, v_ref, o_ref,
                    qa_ref, z_ref, p_ref, acc_ref, ucur_ref, uprev_ref, *, tq, tk, head_dim):
    hp = pl.program_id(1)
    qi = pl.program_id(2)
    pair = 2 * head_dim
    ratio = tq // tk
    last = (qi + 1) * ratio - 1
    lane = lax.broadcasted_iota(jnp.int32, (1, pair), 1)
    row = lax.broadcasted_iota(jnp.int32, (2 * tq, tk), 0)
    col = lax.broadcasted_iota(jnp.int32, (2 * tq, tk), 1)
    q_pos = jnp.where(row >= tq, row - tq, row)
    causal = [col + (ratio - 1 - t) * tk < q_pos for t in range(ratio)]
    tri = _suffix_matrix(tk)
    ones = jnp.ones((tk, pair), BF16)

    q = q_ref[0]
    for h in range(2):
        qa_ref[h * tq:(h + 1) * tq, 0:pair] = jnp.where(
            (lane >= h * head_dim) & (lane < (h + 1) * head_dim), q, jnp.zeros_like(q))
        b = bias_ref[2 * hp + h]
        b1 = b.astype(BF16).astype(F32)
        b2 = (b - b1).astype(BF16).astype(F32)
        b3 = (b - b1 - b2).astype(BF16).astype(F32)
        offset = jnp.where(lane == 0, b1, jnp.where(lane == 1, b2, jnp.where(lane == 2, b3, 0.0)))
        qa_ref[h * tq:(h + 1) * tq, pair:2 * pair] = jnp.broadcast_to(offset, (tq, pair)).astype(BF16)

    def logits(t, slot):
        off = pl.multiple_of(jnp.maximum(last - t, 0) * tk, tk)
        ka = jnp.concatenate([k_ref[0, pl.ds(off, tk), :], ones], axis=1)
        z_ref[slot] = _dot_nt(qa_ref[...], ka)

    def weights(slot, valid):
        right = _sb_suffix(z_ref[slot], tri, valid)
        p_ref[slot] = _sb_weights(z_ref[slot], right, valid)
        return right[:, 0:1]

    def values(t, slot):
        off = pl.multiple_of((last - t) * tk, tk)
        acc_ref[...] += jnp.exp(uprev_ref[...]) * _dot(p_ref[slot], v_ref[0, pl.ds(off, tk), :])

    def stage(t, slot, valid):
        values(t - 1, 1 - slot)
        total = weights(slot, valid)
        logits(t + 1, 1 - slot)
        ucur = ucur_ref[...]
        uprev_ref[...] = ucur
        ucur_ref[...] = ucur - total

    logits(0, 0)
    logits(1, 1)
    ucur_ref[...] = -weights(0, causal[0])
    uprev_ref[...] = jnp.zeros_like(uprev_ref)
    acc_ref[...] = jnp.zeros_like(acc_ref)
    for t in range(1, ratio):
        stage(t, t % 2, causal[t])

    def two_stages(u, carry):
        stage(ratio + 2 * u, 0, None)
        stage(ratio + 2 * u + 1, 1, None)
        return carry

    lax.fori_loop(0, qi * (ratio // 2), two_stages, 0)
    values(last, 1)
    o_ref[0] = jnp.where(lane < head_dim, acc_ref[0:tq], acc_ref[tq:2 * tq]).astype(o_ref.dtype)


def _sb_attention_prompt(q, k, v, bias):
    n, s, d_att = q.shape
    head_dim = d_att // N_HEADS
    tk = ATT_KEY_TILE
    tq = ATT_QUERY_TILE
    assert s % tq == 0 and tq % (2 * tk) == 0
    pair = 2 * head_dim
    grid_spec = pltpu.PrefetchScalarGridSpec(
        num_scalar_prefetch=1,
        grid=(n, N_HEADS // 2, s // tq),
        in_specs=[pl.BlockSpec((1, tq, pair), lambda b, hp, qi, bias: (b, qi, hp)),
                  pl.BlockSpec((1, s, pair), lambda b, hp, qi, bias: (b, 0, hp)),
                  pl.BlockSpec((1, s, pair), lambda b, hp, qi, bias: (b, 0, hp))],
        out_specs=pl.BlockSpec((1, tq, pair), lambda b, hp, qi, bias: (b, qi, hp)),
        scratch_shapes=[pltpu.VMEM((2 * tq, 2 * pair), BF16),
                        pltpu.VMEM((2, 2 * tq, tk), F32),
                        pltpu.VMEM((2, 2 * tq, tk), BF16),
                        pltpu.VMEM((2 * tq, pair), F32),
                        pltpu.VMEM((2 * tq, 1), F32),
                        pltpu.VMEM((2 * tq, 1), F32)])
    return pl.pallas_call(
        functools.partial(_sb_prompt_body, tq=tq, tk=tk, head_dim=head_dim),
        grid_spec=grid_spec,
        out_shape=jax.ShapeDtypeStruct((n, s, d_att), BF16),
        compiler_params=_params(3),
        name="sb_prompt",
    )(bias, q, k, v)


def _sb_sample_body(pt_ref, qbd_ref, bias_ref, knew_ref, vnew_ref, pool_k_ref, pool_v_ref, o_ref,
                    kpage_ref, vpage_ref, sem, kt_ref, vt_ref, acc_ref, used_ref, *, page, t_new, n_steps):
    b = pl.program_id(0)
    g = pl.program_id(1)
    n_seq = pl.num_programs(0)
    n_pages = n_steps * PAGES_PER_STEP
    rows, d_att = acc_ref.shape
    head_dim = d_att // N_HEADS
    tri = _suffix_matrix(page)
    qbd = qbd_ref[0]
    bias = bias_ref[...]
    slot = g % 2

    def page_copies(seq, step, into):
        copies = []
        for i in range(PAGES_PER_STEP):
            page_id = pt_ref[seq, n_pages - 1 - (step * PAGES_PER_STEP + i)]
            copies.append(pltpu.make_async_copy(pool_k_ref.at[page_id], kpage_ref.at[into, i], sem.at[into]))
            copies.append(pltpu.make_async_copy(pool_v_ref.at[page_id], vpage_ref.at[into, i], sem.at[into]))
        return copies

    @pl.when((b == 0) & (g == 0))
    def _():
        for copy in page_copies(b, g, slot):
            copy.start()

    @pl.when((b < n_seq - 1) | (g < n_steps - 1))
    def _():
        wrap = g == n_steps - 1
        for copy in page_copies(jnp.where(wrap, b + 1, b), jnp.where(wrap, 0, g + 1), 1 - slot):
            copy.start()

    for copy in page_copies(b, g, slot):
        copy.wait()

    @pl.when(g == 0)
    def _():
        qidx = lax.broadcasted_iota(jnp.int32, (rows, page), 0) // N_HEADS
        kidx = lax.broadcasted_iota(jnp.int32, (rows, page), 1)
        valid = kidx < qidx
        z = _dot(qbd, knew_ref[0].astype(BF16)) + bias
        right = _sb_suffix(z, tri, valid)
        acc_ref[...] = _dot_nt(_sb_weights(z, right, valid), vnew_ref[0].astype(BF16))
        used_ref[...] = right[:, 0:1]

    for i in range(PAGES_PER_STEP):
        kt_ref[:, i * page:(i + 1) * page] = kpage_ref[slot, i].astype(BF16)
        vt_ref[:, i * page:(i + 1) * page] = vpage_ref[slot, i].astype(BF16)
    z_wide = _dot(qbd, kt_ref[...])
    z = jnp.concatenate([z_wide[:, i * page:(i + 1) * page] + bias for i in range(PAGES_PER_STEP)], axis=0)
    right = _sb_suffix(z, tri, None)
    used = used_ref[...]
    used_rows = []
    for i in range(PAGES_PER_STEP):
        used_rows.append(used)
        used = used + right[i * rows:(i + 1) * rows, 0:1]
    used_ref[...] = used
    p = _sb_weights(z, right + jnp.concatenate(used_rows, axis=0), None)
    p_wide = jnp.concatenate([p[i * rows:(i + 1) * rows] for i in range(PAGES_PER_STEP)], axis=1)
    acc_ref[...] += _dot_nt(p_wide, vt_ref[...])

    @pl.when(g == n_steps - 1)
    def _():
        head = lax.broadcasted_iota(jnp.int32, (rows, d_att), 0) % N_HEADS
        lane_head = lax.broadcasted_iota(jnp.int32, (rows, d_att), 1) // head_dim
        own = jnp.where(head == lane_head, acc_ref[...], 0.0)
        o_ref[0] = jnp.sum(own.reshape(t_new, N_HEADS, d_att), axis=1)


def _sb_attention_sample(q, kt_new, vt_new, cache_k, cache_v, page_table, bias):
    nb, t_new, d_att = q.shape
    head_dim = d_att // N_HEADS
    page = cache_k.shape[2]
    n_pages = page_table.shape[1]
    n_steps = n_pages // PAGES_PER_STEP
    rows = t_new * N_HEADS
    head_mask = (jnp.arange(d_att)[None, :] // head_dim == jnp.arange(N_HEADS)[:, None]).astype(q.dtype)
    qbd = (q[:, :, None, :] * head_mask[None, None]).reshape(nb, rows, d_att)
    bias_rows = jnp.broadcast_to(jnp.tile(bias.astype(F32), t_new)[:, None], (rows, page))
    new_keys = lambda a: jnp.pad(a, ((0, 0), (0, 0), (0, page - t_new)))

    assert n_steps % 2 == 0, "the page double buffer alternates by step parity"
    seq_spec = lambda r, c: pl.BlockSpec((1, r, c), lambda b, g, pt: (b, 0, 0))
    pool_spec = pl.BlockSpec(memory_space=pl.ANY)
    grid_spec = pltpu.PrefetchScalarGridSpec(
        num_scalar_prefetch=1,
        grid=(nb, n_steps),
        in_specs=[seq_spec(rows, d_att), pl.BlockSpec((rows, page), lambda b, g, pt: (0, 0)),
                  seq_spec(d_att, page), seq_spec(d_att, page), pool_spec, pool_spec],
        out_specs=seq_spec(t_new, d_att),
        scratch_shapes=[pltpu.VMEM((2, PAGES_PER_STEP, d_att, page), F32),
                        pltpu.VMEM((2, PAGES_PER_STEP, d_att, page), F32),
                        pltpu.SemaphoreType.DMA((2,)),
                        pltpu.VMEM((d_att, PAGES_PER_STEP * page), BF16),
                        pltpu.VMEM((d_att, PAGES_PER_STEP * page), BF16),
                        pltpu.VMEM((rows, d_att), F32), pltpu.VMEM((rows, 1), F32)])
    return pl.pallas_call(
        functools.partial(_sb_sample_body, page=page, t_new=t_new, n_steps=n_steps),
        grid_spec=grid_spec,
        out_shape=jax.ShapeDtypeStruct((nb, t_new, d_att), F32),
        compiler_params=_params(2),
        name="sb_sample",
    )(page_table, qbd, bias_rows, new_keys(kt_new), new_keys(vt_new), cache_k, cache_v)


def _conv_body(prev_ref, cur_ref, w_ref, b_ref, o_ref, ext_ref, *, width, tiles_per_seq, chunk):
    bs, tm, _ = cur_ref.shape
    span = ext_ref.shape[2]
    prev = prev_ref[...]
    if tiles_per_seq is not None:
        prev = jnp.where(pl.program_id(0) % tiles_per_seq == 0, 0.0, prev)
    ext_ref[0, :, 0:HALO, :] = prev
    ext_ref[0, :, HALO:HALO + tm, :] = cur_ref[...]
    for s in range(1, SUBLANES):
        ext_ref[s, :, 0:span - SUBLANES, :] = ext_ref[0, :, s:s + span - SUBLANES, :]
    first = HALO - (width - 1)
    for b in range(bs):
        for r0 in range(0, tm, chunk):
            acc = None
            for w in range(width):
                shift = (first + w) % SUBLANES
                base = first + w - shift + r0
                term = ext_ref[shift, b, base:base + chunk, :] * w_ref[w:w + 1, :]
                acc = term if acc is None else acc + term
            o_ref[b, r0:r0 + chunk, :] = acc + b_ref[...]


def _depthwise_conv(prev, cur, w_dw, b_dw, *, bs, tm, tiles_per_seq):
    n, rows, c = cur.shape
    width = w_dw.shape[0]
    w_pad = jnp.pad(w_dw, ((0, HALO - width), (0, 0)))
    if tiles_per_seq is None:
        grid = (n // bs,)
        prev_index = lambda i: (i, 0, 0)
        cur_index = lambda i: (i, 0, 0)
    else:
        grid = (n * tiles_per_seq,)
        halo_per_tile = tm // HALO
        prev_index = lambda i: (jnp.maximum(i * halo_per_tile - 1, 0), 0, 0)
        cur_index = lambda i: (i // tiles_per_seq, i % tiles_per_seq, 0)
    return pl.pallas_call(
        functools.partial(_conv_body, width=width, tiles_per_seq=tiles_per_seq, chunk=min(64, tm)),
        grid=grid,
        in_specs=[pl.BlockSpec((bs, HALO, c), prev_index),
                  pl.BlockSpec((bs, tm, c), cur_index),
                  pl.BlockSpec((HALO, c), lambda i: (0, 0)),
                  pl.BlockSpec((1, c), lambda i: (0, 0))],
        out_specs=pl.BlockSpec((bs, tm, c), cur_index),
        out_shape=jax.ShapeDtypeStruct((n, rows, c), F32),
        scratch_shapes=[pltpu.VMEM((SUBLANES, bs, HALO + tm, c), F32)],
        compiler_params=_params(1),
        name="conv",
    )(prev, cur, w_pad, b_dw.reshape(1, c))


def _merge_body(att_ref, dw_ref, ga_ref, gc_ref, x_ref, gt1_ref, sc2_ref, sh2_ref,
                w_att_ref, ln_g_ref, ln_b_ref, w_conv_ref, b_conv_ref, w_out_ref,
                g_post_ref, g_pre_ref, w_router_ref, b_router_ref,
                x1_ref, h2_ref, gate_ref, expert_ref):
    y_att = _dot(att_ref[...].astype(BF16), w_att_ref[...])
    dw = dw_ref[...]
    mu = jnp.mean(dw, axis=-1, keepdims=True)
    xc = dw - mu
    ln = xc * lax.rsqrt(jnp.mean(xc * xc, axis=-1, keepdims=True) + LN_EPS) * ln_g_ref[...] + ln_b_ref[...]
    act = ln * _sigmoid(ln)
    y_conv = _dot(act.astype(BF16), w_conv_ref[...]) + b_conv_ref[...]
    mixed = ga_ref[...] * y_att + gc_ref[...] * y_conv
    out = _dot(mixed.astype(BF16), w_out_ref[...])
    x1 = x_ref[...] + gt1_ref[...] * _rms(out, g_post_ref[...])
    x1_ref[...] = x1
    h2 = _rms(x1, g_pre_ref[...]) * (1.0 + sc2_ref[...]) + sh2_ref[...]
    h2_ref[...] = h2
    logits = _dot(h2.astype(BF16), w_router_ref[...]) + b_router_ref[...]

    n_exp = logits.shape[1]
    lane = lax.broadcasted_iota(jnp.int32, logits.shape, 1)
    out_lane = lax.broadcasted_iota(jnp.int32, gate_ref.shape, 1)
    experts = jnp.zeros(expert_ref.shape, jnp.int32)
    weights = jnp.zeros(gate_ref.shape, F32)
    top = None
    for k in range(TOP_K):
        best = jnp.max(logits, axis=-1, keepdims=True)
        idx = jnp.min(jnp.where(logits == best, lane, n_exp), axis=-1, keepdims=True)
        top = best if top is None else top
        experts = jnp.where(out_lane == k, idx, experts)
        weights = jnp.where(out_lane == k, jnp.exp(best - top), weights)
        logits = jnp.where(lane == idx, -jnp.inf, logits)
    gate_ref[...] = weights / jnp.sum(weights, axis=-1, keepdims=True)
    expert_ref[...] = experts


def _merge(att, dw, ga, gc, x, gt1, sc2, sh2, weights, *, per_token, rows_per_seq):
    rows, d = x.shape
    tm = min(TOKEN_TILE, rows)
    row_spec = lambda a: pl.BlockSpec((tm, a.shape[1]), lambda i: (i, 0))
    mod_spec = _mod_spec(per_token, tm, d, rows_per_seq)
    full_spec = lambda a: pl.BlockSpec(a.shape, lambda i: (0, 0))
    out_widths = (d, d, LANES, LANES)
    out_dtypes = (F32, F32, F32, jnp.int32)
    return pl.pallas_call(
        _merge_body,
        grid=(rows // tm,),
        in_specs=[row_spec(a) for a in (att, dw, ga, gc, x)] + [mod_spec] * 3 + [full_spec(w) for w in weights],
        out_specs=[pl.BlockSpec((tm, w), lambda i: (i, 0)) for w in out_widths],
        out_shape=[jax.ShapeDtypeStruct((rows, w), t) for w, t in zip(out_widths, out_dtypes)],
        compiler_params=_params(1),
        name="merge",
    )(att, dw, ga, gc, x, gt1, sc2, sh2, *weights)


def _to_bf16_body(w_ref, o_ref):
    o_ref[...] = w_ref[...].astype(BF16)


def _to_bf16(w):
    n_exp, rows, cols = w.shape
    tr = min(512, rows)
    spec = pl.BlockSpec((None, tr, cols), lambda e, r: (e, r, 0))
    return pl.pallas_call(
        _to_bf16_body,
        grid=(n_exp, rows // tr),
        in_specs=[spec],
        out_specs=spec,
        out_shape=jax.ShapeDtypeStruct(w.shape, BF16),
        compiler_params=_params(2),
        name="to_bf16",
    )(w)


def _moe_body(tile_expert_ref, n_tiles_ref, x_ref, w_gu_ref, b_gu_ref, w_down_ref, b_down_ref, o_ref, *, chunk):
    d_expert = w_down_ref.shape[0]

    @pl.when(pl.program_id(0) < n_tiles_ref[0])
    def _():
        x = x_ref[...].astype(BF16)
        acc = None
        for c0 in range(0, d_expert, chunk):
            g = _dot(x, w_gu_ref[:, c0:c0 + chunk]) + b_gu_ref[:, c0:c0 + chunk]
            u = _dot(x, w_gu_ref[:, d_expert + c0:d_expert + c0 + chunk]) + b_gu_ref[:, d_expert + c0:d_expert + c0 + chunk]
            g = jnp.minimum(g, SWIGLU_LIMIT)
            u = jnp.clip(u, -SWIGLU_LIMIT, SWIGLU_LIMIT)
            a = ((u + 1.0) * g * _sigmoid(SWIGLU_ALPHA * g)).astype(BF16)
            part = _dot(a, w_down_ref[c0:c0 + chunk, :])
            acc = part if acc is None else acc + part
        o_ref[...] = acc + b_down_ref[...]

    @pl.when(pl.program_id(0) >= n_tiles_ref[0])
    def _():
        o_ref[...] = jnp.zeros_like(o_ref)


def _moe_experts(x_sorted, tile_expert, n_tiles, w_gu, b_gu, w_down, b_down):
    n_rows, d = x_sorted.shape
    n_exp, _, d_gu = w_gu.shape
    d_expert = w_down.shape[1]
    max_tiles = n_rows // MOE_TILE

    def row_index(i, te, nt):
        return (jnp.minimum(i, nt[0] - 1), 0)

    def expert_index(i, te, nt):
        return (te[i], 0, 0)

    grid_spec = pltpu.PrefetchScalarGridSpec(
        num_scalar_prefetch=2,
        grid=(max_tiles,),
        in_specs=[pl.BlockSpec((MOE_TILE, d), row_index),
                  pl.BlockSpec((None, d, d_gu), expert_index),
                  pl.BlockSpec((None, 1, d_gu), expert_index),
                  pl.BlockSpec((None, d_expert, d), expert_index),
                  pl.BlockSpec((None, 1, d), expert_index)],
        out_specs=pl.BlockSpec((MOE_TILE, d), lambda i, te, nt: (i, 0)))
    return pl.pallas_call(
        functools.partial(_moe_body, chunk=min(512, d_expert)),
        grid_spec=grid_spec,
        out_shape=jax.ShapeDtypeStruct((n_rows, d), F32),
        compiler_params=_params(1),
        name="moe",
    )(tile_expert, n_tiles, x_sorted, w_gu, b_gu.reshape(n_exp, 1, d_gu), w_down, b_down.reshape(n_exp, 1, d))


def _rank_body(expert_ref, rank_ref, count_ref, carry_ref):
    i = pl.program_id(0)
    tm, lanes = expert_ref.shape

    @pl.when(i == 0)
    def _():
        carry_ref[...] = jnp.zeros_like(carry_ref)

    experts = expert_ref[...]
    lane = lax.broadcasted_iota(jnp.int32, (tm, lanes), 1)
    picks = [lane == experts[:, k:k + 1] for k in range(TOP_K)]
    chosen = functools.reduce(jnp.logical_or, picks)
    row = lax.broadcasted_iota(jnp.int32, (tm, tm), 0)
    col = lax.broadcasted_iota(jnp.int32, (tm, tm), 1)
    earlier = (col < row).astype(BF16)
    before = _dot(earlier, jnp.where(chosen, 1.0, 0.0).astype(BF16)) + carry_ref[...]
    rank = jnp.zeros((tm, lanes), F32)
    for k in range(TOP_K):
        rank = jnp.where(lane == k, jnp.sum(jnp.where(picks[k], before, 0.0), axis=-1, keepdims=True), rank)
    rank_ref[...] = rank.astype(jnp.int32)
    carry_ref[...] += jnp.sum(jnp.where(chosen, 1.0, 0.0), axis=0, keepdims=True)
    count_ref[...] = carry_ref[...].astype(jnp.int32)


def _rank(experts):
    n_tok, lanes = experts.shape
    tm = TOKEN_TILE
    return pl.pallas_call(
        _rank_body,
        grid=(n_tok // tm,),
        in_specs=[pl.BlockSpec((tm, lanes), lambda i: (i, 0))],
        out_specs=[pl.BlockSpec((tm, lanes), lambda i: (i, 0)), pl.BlockSpec((1, lanes), lambda i: (0, 0))],
        out_shape=[jax.ShapeDtypeStruct((n_tok, lanes), jnp.int32), jax.ShapeDtypeStruct((1, lanes), jnp.int32)],
        scratch_shapes=[pltpu.VMEM((1, lanes), F32)],
        compiler_params=_params(1),
        name="rank",
    )(experts)


def _layout(counts, n_assign):
    n_exp = counts.shape[0]
    padded = (counts + MOE_TILE - 1) // MOE_TILE * MOE_TILE
    pend = jnp.cumsum(padded)
    max_tiles = (n_assign + MOE_TILE - 1) // MOE_TILE + n_exp
    n_tiles = (pend[-1] // MOE_TILE).astype(jnp.int32)
    tile_id = jnp.minimum(jnp.arange(max_tiles, dtype=jnp.int32), n_tiles - 1)
    tile_expert = jnp.sum((pend[None, :] <= (tile_id * MOE_TILE)[:, None]).astype(jnp.int32), axis=1)
    return pend - padded, jnp.minimum(tile_expert, n_exp - 1), n_tiles.reshape(1), max_tiles * MOE_TILE


def _row_copy(src_ref, src_row, dst_ref, dst_row, sem):
    return pltpu.make_async_copy(src_ref.at[pl.ds(src_row, 1)], dst_ref.at[pl.ds(dst_row, 1)], sem)


def _dispatch_body(pos_ref, h_ref, init_ref, o_ref, sem):
    del init_ref
    tm = h_ref.shape[0]

    def issue(t, carry):
        for k in range(TOP_K):
            _row_copy(h_ref, t, o_ref, pos_ref[0, 0, t * TOP_K + k], sem).start()
        return carry

    lax.fori_loop(0, tm, issue, 0)

    for k in range(TOP_K):
        pltpu.make_async_copy(h_ref, o_ref.at[pl.ds(0, tm)], sem).wait()


def _dispatch(h, pos, buffer):
    n_tok, d = h.shape
    tm = min(TOKEN_TILE, n_tok)
    return pl.pallas_call(
        _dispatch_body,
        grid=(n_tok // tm,),
        in_specs=[pl.BlockSpec((1, 1, tm * TOP_K), lambda i: (i, 0, 0), memory_space=pltpu.SMEM),
                  pl.BlockSpec((tm, d), lambda i: (i, 0)),
                  pl.BlockSpec(memory_space=pl.ANY)],
        out_specs=pl.BlockSpec(memory_space=pl.ANY),
        out_shape=jax.ShapeDtypeStruct(buffer.shape, buffer.dtype),
        scratch_shapes=[pltpu.SemaphoreType.DMA(())],
        input_output_aliases={2: 0},
        compiler_params=_params(1),
        name="dispatch",
    )(pos.reshape(n_tok // tm, 1, tm * TOP_K), h, buffer)


def _combine_body(pos_ref, y_ref, gate_ref, x1_ref, gt2_ref, g_ref, o_ref, buf_ref, sem):
    tm = x1_ref.shape[0]

    def issue(t, carry):
        for k in range(TOP_K):
            _row_copy(y_ref, pos_ref[0, 0, t * TOP_K + k], buf_ref.at[k], t, sem).start()
        return carry

    lax.fori_loop(0, tm, issue, 0)

    for k in range(TOP_K):
        pltpu.make_async_copy(y_ref.at[pl.ds(0, tm)], buf_ref.at[k], sem).wait()
    gates = gate_ref[...]
    y = gates[:, 0:1] * buf_ref[0]
    for k in range(1, TOP_K):
        y += gates[:, k:k + 1] * buf_ref[k]
    o_ref[...] = x1_ref[...] + gt2_ref[...] * _rms(y, g_ref[...])


def _combine(y_sorted, pos, gates, x1, gt2, g, *, per_token, rows_per_seq):
    rows, d = x1.shape
    tm = min(TOKEN_TILE, rows)
    row_spec = lambda w: pl.BlockSpec((tm, w), lambda i: (i, 0))
    return pl.pallas_call(
        _combine_body,
        grid=(rows // tm,),
        in_specs=[pl.BlockSpec((1, 1, tm * TOP_K), lambda i: (i, 0, 0), memory_space=pltpu.SMEM),
                  pl.BlockSpec(memory_space=pl.ANY),
                  row_spec(gates.shape[1]), row_spec(d), _mod_spec(per_token, tm, d, rows_per_seq),
                  pl.BlockSpec((1, d), lambda i: (0, 0))],
        out_specs=row_spec(d),
        out_shape=jax.ShapeDtypeStruct((rows, d), F32),
        scratch_shapes=[pltpu.VMEM((TOP_K, tm, d), F32), pltpu.SemaphoreType.DMA(())],
        compiler_params=_params(1),
        name="combine",
    )(pos.reshape(rows // tm, 1, tm * TOP_K), y_sorted, gates, x1, gt2, g.reshape(1, d))


def kernel(x_prompt, x_sample, cache_k, cache_v, state_conv, page_table, c_prompt, c_sample, w_mod, b_mod, g_pre_mix, g_post_mix, w_in, b_sb, w_att_out, w_dw, b_dw, ln_conv_g, ln_conv_b, w_conv_out, b_conv_out, w_out, g_pre_ffn, g_post_ffn, w_router, b_router, w_gu, b_gu, w_down, b_down):
    depth = w_mod.shape[0]
    assert depth == 1, "single-layer trunk"
    n, s, d = x_prompt.shape
    nb, t_new, _ = x_sample.shape
    d_att = w_att_out.shape[1]
    d_conv = w_dw.shape[2]
    width = w_dw.shape[1]
    n_exp = w_router.shape[2]
    head_dim = d_att // N_HEADS
    page = cache_k.shape[2]
    assert width - 1 <= HALO and s % TOKEN_TILE == 0
    l = 0

    row2 = lambda a: a.reshape(1, -1)
    w_in_b = w_in[l].astype(BF16)
    w_kvt = w_in_b[:, d_att:3 * d_att].T
    merge_w = (w_att_out[l].astype(BF16), row2(ln_conv_g[l]), row2(ln_conv_b[l]),
               w_conv_out[l].astype(BF16), row2(b_conv_out[l]), w_out[l].astype(BF16),
               row2(g_post_mix[l]), row2(g_pre_ffn[l]), w_router[l].astype(BF16), row2(b_router[l]))

    n_seq = n + nb
    c_all = jnp.pad(jnp.concatenate([c_prompt, c_sample], axis=0), ((0, -n_seq % 8), (0, 0)))
    mod = _modulation(c_all, w_mod[l], b_mod[l])
    mod_p = [m.reshape(n, 1, d) for m in jnp.split(mod[:n], 6, axis=-1)]
    mod_s = jnp.split(jnp.repeat(mod[n:n_seq], t_new, axis=0), 6, axis=-1)

    def mixing(x, mods, attend, conv, per_token, rows_per_seq, kv_group):
        sh1, sc1, gt1, sh2, sc2, _ = mods
        q, kt, vt, kb, vb, u, ga, gc = _inproj(x, sc1, sh1, g_pre_mix[l], w_in_b, w_kvt, d_att=d_att, d_conv=d_conv,
                                               per_token=per_token, rows_per_seq=rows_per_seq, kv_group=kv_group)
        att = attend(q, kt, vt, kb, vb)
        dw = conv(u)
        x1, h2, gates, experts = _merge(att, dw, ga, gc, x, gt1, sc2, sh2, merge_w,
                                        per_token=per_token, rows_per_seq=rows_per_seq)
        return x1, h2, gates, experts, kt, vt, u

    def attend_prompt(q, kt, vt, kb, vb):
        shape = (n, s, d_att)
        return _sb_attention_prompt(q.reshape(shape), kb.reshape(shape), vb.reshape(shape),
                                    b_sb[l].astype(F32)).reshape(n * s, d_att)

    def conv_prompt(u):
        tm = min(512, s)
        dw = _depthwise_conv(u.reshape(n * s // HALO, HALO, d_conv), u.reshape(n, s, d_conv), w_dw[l], b_dw[l],
                             bs=1, tm=tm, tiles_per_seq=s // tm)
        return dw.reshape(n * s, d_conv)

    x1_p, h2_p, gates_p, experts_p, kt_p, vt_p, u_p = mixing(x_prompt.reshape(n * s, d), mod_p, attend_prompt,
                                                             conv_prompt, False, s, s)

    pool_k = cache_k[l].transpose(0, 2, 3, 1).reshape(-1, d_att, page)
    pool_v = cache_v[l].transpose(0, 2, 3, 1).reshape(-1, d_att, page)

    def attend_sample(q, kt, vt, kb, vb):
        per_seq = lambda a: a.reshape(d_att, nb, t_new).transpose(1, 0, 2)
        return _sb_attention_sample(q.reshape(nb, t_new, d_att), per_seq(kt), per_seq(vt), pool_k, pool_v,
                                    page_table, b_sb[l]).reshape(nb * t_new, d_att)

    def conv_sample(u):
        t_pad = -t_new % 8
        cur = jnp.pad(u.reshape(nb, t_new, d_conv), ((0, 0), (0, t_pad), (0, 0)))
        prev = jnp.pad(state_conv[l], ((0, 0), (HALO - (width - 1), 0), (0, 0)))
        dw = _depthwise_conv(prev, cur, w_dw[l], b_dw[l], bs=min(16, nb), tm=t_new + t_pad, tiles_per_seq=None)
        return dw[:, :t_new].reshape(nb * t_new, d_conv)

    x1_s, h2_s, gates_s, experts_s, kt_s, vt_s, u_s = mixing(x_sample.reshape(nb * t_new, d), mod_s, attend_sample,
                                                             conv_sample, True, t_new, nb * t_new)

    n_p = n * s
    n_tok = n_p + nb * t_new
    assert n_p % TOKEN_TILE == 0 and n_tok % TOKEN_TILE == 0
    experts = jnp.concatenate([experts_p, experts_s], axis=0)
    rank, counts = _rank(experts)
    row_start, tile_expert, n_tiles, n_rows = _layout(counts[0, :n_exp], n_tok * TOP_K)
    pos = row_start[experts[:, :TOP_K]] + rank[:, :TOP_K]
    x_sorted = _dispatch(h2_p, pos[:n_p], jnp.zeros((n_rows, d), F32))
    x_sorted = _dispatch(h2_s, pos[n_p:], x_sorted)
    y_sorted = _moe_experts(x_sorted, tile_expert, n_tiles, _to_bf16(w_gu[l]), b_gu[l], _to_bf16(w_down[l]), b_down[l])
    y_prompt = _combine(y_sorted, pos[:n_p], gates_p, x1_p, mod_p[5], g_post_ffn[l], per_token=False, rows_per_seq=s)
    y_sample = _combine(y_sorted, pos[n_p:], gates_s, x1_s, mod_s[5], g_post_ffn[l], per_token=True,
                        rows_per_seq=t_new)

    heads_last = lambda a, seqs, toks: a.reshape(N_HEADS, head_dim, seqs, toks).transpose(2, 3, 0, 1)[None]
    prompt_kv = lambda a: heads_last(a.transpose(1, 0, 2), n, s)
    sample_kv = lambda a: heads_last(a[0], nb, t_new)
    keep = width - 1
    conv_p = u_p.reshape(n, s, d_conv)[:, s - keep:]
    conv_s = jnp.concatenate([state_conv[l], u_s.reshape(nb, t_new, d_conv)], axis=1)[:, -keep:]
    return (y_prompt.reshape(n, s, d), y_sample.reshape(nb, t_new, d),
            prompt_kv(kt_p), prompt_kv(vt_p), conv_p[None],
            sample_kv(kt_s), sample_kv(vt_s), conv_s[None])
```

```python
import functools

import jax
import jax.numpy as jnp
from jax import lax
from jax.experimental import pallas as pl
from jax.experimental.pallas import tpu as pltpu

F32 = jnp.float32
BF16 = jnp.bfloat16

N_HEADS = 8
TOP_K = 4
SWIGLU_LIMIT = 7.0
SWIGLU_ALPHA = 1.702
RMS_EPS = 1e-6
LN_EPS = 1e-5
SOFTPLUS_LINEAR_ABOVE = 30.0

VMEM_LIMIT_BYTES = 48 * 1024 * 1024
HALO = 32
TOKEN_TILE = 256
ATT_KEY_TILE = 256
ATT_QUERY_TILE = 512
MOE_TILE = 512
LANES = 128
SUBLANES = 8
DMA_QUEUES = 2
PAGES_PER_STEP = 8


def _params(n_axes):
    return pltpu.CompilerParams(dimension_semantics=("arbitrary",) * n_axes,
                                vmem_limit_bytes=VMEM_LIMIT_BYTES)


def _dot(a, b):
    return jnp.dot(a, b, preferred_element_type=F32)


def _dot_nt(a, b):
    return lax.dot_general(a, b, (((1,), (1,)), ((), ())), preferred_element_type=F32)


def _rms(x, g):
    return x * lax.rsqrt(jnp.mean(x * x, axis=-1, keepdims=True) + RMS_EPS) * g


def _sigmoid(x):
    return 1.0 / (1.0 + jnp.exp(-x))


def _mod_body(c_ref, w_ref, b_ref, o_ref):
    c = c_ref[...]
    s = (c * _sigmoid(c)).astype(BF16)
    o_ref[...] = _dot(s, w_ref[...].astype(BF16)) + b_ref[...]


def _modulation(c, w_mod, b_mod):
    n, d = c.shape
    n_out = w_mod.shape[1]
    tn = n_out // 6
    return pl.pallas_call(
        _mod_body,
        grid=(n_out // tn,),
        in_specs=[pl.BlockSpec((n, d), lambda j: (0, 0)),
                  pl.BlockSpec((d, tn), lambda j: (0, j)),
                  pl.BlockSpec((1, tn), lambda j: (0, j))],
        out_specs=pl.BlockSpec((n, tn), lambda j: (0, j)),
        out_shape=jax.ShapeDtypeStruct((n, n_out), F32),
        compiler_params=_params(1),
        name="mod",
    )(c, w_mod, b_mod.reshape(1, n_out))


def _inproj_body(x_ref, sc_ref, sh_ref, g_ref, w_ref, w_kvt_ref,
                 q_ref, kt_ref, vt_ref, kb_ref, vb_ref, u_ref, ga_ref, gc_ref, *, d_att, d_conv, q_scale):
    d_model = x_ref.shape[-1]
    h = _rms(x_ref[...], g_ref[...]) * (1.0 + sc_ref[...]) + sh_ref[...]
    hb = h.astype(BF16)

    def proj(lo, width):
        return _dot(hb, w_ref[:, lo:lo + width])

    q_ref[...] = (proj(0, d_att) * q_scale).astype(BF16)
    kb_ref[...] = proj(d_att, d_att).astype(BF16)
    vb_ref[...] = proj(2 * d_att, d_att).astype(BF16)
    kt_ref[...] = _dot_nt(w_kvt_ref[0:d_att, :], hb)
    vt_ref[...] = _dot_nt(w_kvt_ref[d_att:2 * d_att, :], hb)
    ca = proj(3 * d_att, d_conv)
    cb = proj(3 * d_att + d_conv, d_conv)
    u_ref[...] = ca * _sigmoid(cb)
    ga_ref[...] = _sigmoid(proj(3 * d_att + 2 * d_conv, d_model))
    gc_ref[...] = _sigmoid(proj(3 * d_att + 2 * d_conv + d_model, d_model))


def _mod_spec(per_token, tm, d, rows_per_seq):
    if per_token:
        return pl.BlockSpec((tm, d), lambda i: (i, 0))
    tiles_per_seq = rows_per_seq // tm
    return pl.BlockSpec((None, 1, d), lambda i: (i // tiles_per_seq, 0, 0))


def _inproj(x, sc, sh, g, w_in, w_kvt, *, d_att, d_conv, per_token, rows_per_seq, kv_group):
    rows, d = x.shape
    tm = min(TOKEN_TILE, rows)
    head_dim = d_att // N_HEADS
    tiles_per_group = kv_group // tm
    row_spec = lambda width: pl.BlockSpec((tm, width), lambda i: (i, 0))
    row_out = lambda width, dtype: (row_spec(width), jax.ShapeDtypeStruct((rows, width), dtype))
    kvt_out = (pl.BlockSpec((None, d_att, tm), lambda i: (i // tiles_per_group, 0, i % tiles_per_group)),
               jax.ShapeDtypeStruct((rows // kv_group, d_att, kv_group), F32))
    mod_spec = _mod_spec(per_token, tm, d, rows_per_seq)
    outs = (row_out(d_att, BF16), kvt_out, kvt_out, row_out(d_att, BF16), row_out(d_att, BF16),
            row_out(d_conv, F32), row_out(d, F32), row_out(d, F32))
    return pl.pallas_call(
        functools.partial(_inproj_body, d_att=d_att, d_conv=d_conv, q_scale=head_dim ** -0.5),
        grid=(rows // tm,),
        in_specs=[row_spec(d), mod_spec, mod_spec,
                  pl.BlockSpec((1, d), lambda i: (0, 0)),
                  pl.BlockSpec(w_in.shape, lambda i: (0, 0)),
                  pl.BlockSpec(w_kvt.shape, lambda i: (0, 0))],
        out_specs=[spec for spec, _ in outs],
        out_shape=[shape for _, shape in outs],
        compiler_params=_params(1),
        name="inproj",
    )(x, sc, sh, g.reshape(1, d), w_in, w_kvt)


def _suffix_matrix(tile):
    row = lax.broadcasted_iota(jnp.int32, (tile, tile), 0)
    col = lax.broadcasted_iota(jnp.int32, (tile, tile), 1)
    return (row >= col).astype(BF16)


def _sb_suffix(z, tri, valid):
    sp = jnp.where(z > SOFTPLUS_LINEAR_ABOVE, z, jnp.log(1.0 + jnp.exp(z)))
    if valid is not None:
        sp = jnp.where(valid, sp, 0.0)
    return _dot(sp.astype(BF16), tri)


def _sb_weights(z, right, valid):
    p = jnp.exp(z - right)
    if valid is not None:
        p = jnp.where(valid, p, 0.0)
    return p.astype(BF16)


def _sb_prompt_body(bias_ref, q_ref, k_ref, v_ref, o_ref,
                    qa_ref, z_ref, p_ref, acc_ref, ucur_ref, uprev_ref, *, tq, tk, head_dim):
    hp = pl.program_id(1)
    qi = pl.program_id(2)
    pair = 2 * head_dim
    ratio = tq // tk
    last = (qi + 1) * ratio - 1
    lane = lax.broadcasted_iota(jnp.int32, (1, pair), 1)
    row = lax.broadcasted_iota(jnp.int32, (2 * tq, tk), 0)
    col = lax.broadcasted_iota(jnp.int32, (2 * tq, tk), 1)
    q_pos = jnp.where(row >= tq, row - tq, row)
    causal = [col + (ratio - 1 - t) * tk < q_pos for t in range(ratio)]
    tri = _suffix_matrix(tk)
    ones = jnp.ones((tk, pair), BF16)

    q = q_ref[0]
    for h in range(2):
        qa_ref[h * tq:(h + 1) * tq, 0:pair] = jnp.where(
            (lane >= h * head_dim) & (lane < (h + 1) * head_dim), q, jnp.zeros_like(q))
        b = bias_ref[2 * hp + h]
        b1 = b.astype(BF16).astype(F32)
        b2 = (b - b1).astype(BF16).astype(F32)
        b3 = (b - b1 - b2).astype(BF16).astype(F32)
        offset = jnp.where(lane == 0, b1, jnp.where(lane == 1, b2, jnp.where(lane == 2, b3, 0.0)))
        qa_ref[h * tq:(h + 1) * tq, pair:2 * pair] = jnp.broadcast_to(offset, (tq, pair)).astype(BF16)

    def logits(t, slot):
        off = pl.multiple_of(jnp.maximum(last - t, 0) * tk, tk)
        ka = jnp.concatenate([k_ref[0, pl.ds(off, tk), :], ones], axis=1)
        z_ref[slot] = _dot_nt(qa_ref[...], ka)

    def weights(slot, valid):
        right = _sb_suffix(z_ref[slot], tri, valid)
        p_ref[slot] = _sb_weights(z_ref[slot], right, valid)
        return right[:, 0:1]

    def values(t, slot):
        off = pl.multiple_of((last - t) * tk, tk)
        acc_ref[...] += jnp.exp(uprev_ref[...]) * _dot(p_ref[slot], v_ref[0, pl.ds(off, tk), :])

    def stage(t, slot, valid):
        values(t - 1, 1 - slot)
        total = weights(slot, valid)
        logits(t + 1, 1 - slot)
        ucur = ucur_ref[...]
        uprev_ref[...] = ucur
        ucur_ref[...] = ucur - total

    logits(0, 0)
    logits(1, 1)
    ucur_ref[...] = -weights(0, causal[0])
    uprev_ref[...] = jnp.zeros_like(uprev_ref)
    acc_ref[...] = jnp.zeros_like(acc_ref)
    for t in range(1, ratio):
        stage(t, t % 2, causal[t])

    def two_stages(u, carry):
        stage(ratio + 2 * u, 0, None)
        stage(ratio + 2 * u + 1, 1, None)
        return carry

    lax.fori_loop(0, qi * (ratio // 2), two_stages, 0)
    values(last, 1)
    o_ref[0] = jnp.where(lane < head_dim, acc_ref[0:tq], acc_ref[tq:2 * tq]).astype(o_ref.dtype)


def _sb_attention_prompt(q, k, v, bias):
    n, s, d_att = q.shape
    head_dim = d_att // N_HEADS
    tk = ATT_KEY_TILE
    tq = ATT_QUERY_TILE
    assert s % tq == 0 and tq % (2 * tk) == 0
    pair = 2 * head_dim
    grid_spec = pltpu.PrefetchScalarGridSpec(
        num_scalar_prefetch=1,
        grid=(n, N_HEADS // 2, s // tq),
        in_specs=[pl.BlockSpec((1, tq, pair), lambda b, hp, qi, bias: (b, qi, hp)),
                  pl.BlockSpec((1, s, pair), lambda b, hp, qi, bias: (b, 0, hp)),
                  pl.BlockSpec((1, s, pair), lambda b, hp, qi, bias: (b, 0, hp))],
        out_specs=pl.BlockSpec((1, tq, pair), lambda b, hp, qi, bias: (b, qi, hp)),
        scratch_shapes=[pltpu.VMEM((2 * tq, 2 * pair), BF16),
                        pltpu.VMEM((2, 2 * tq, tk), F32),
                        pltpu.VMEM((2, 2 * tq, tk), BF16),
                        pltpu.VMEM((2 * tq, pair), F32),
                        pltpu.VMEM((2 * tq, 1), F32),
                        pltpu.VMEM((2 * tq, 1), F32)])
    return pl.pallas_call(
        functools.partial(_sb_prompt_body, tq=tq, tk=tk, head_dim=head_dim),
        grid_spec=grid_spec,
        out_shape=jax.ShapeDtypeStruct((n, s, d_att), BF16),
        compiler_params=_params(3),
        name="sb_prompt",
    )(bias, q, k, v)


def _sb_sample_body(pt_ref, qbd_ref, bias_ref, knew_ref, vnew_ref, pool_k_ref, pool_v_ref, o_ref,
                    kpage_ref, vpage_ref, sem, kt_ref, vt_ref, acc_ref, used_ref, *, page, t_new, n_steps):
    b = pl.program_id(0)
    g = pl.program_id(1)
    n_seq = pl.num_programs(0)
    n_pages = n_steps * PAGES_PER_STEP
    rows, d_att = acc_ref.shape
    head_dim = d_att // N_HEADS
    tri = _suffix_matrix(page)
    qbd = qbd_ref[0]
    bias = bias_ref[...]
    slot = g % 2

    def page_copies(seq, step, into):
        copies = []
        for i in range(PAGES_PER_STEP):
            page_id = pt_ref[seq, n_pages - 1 - (step * PAGES_PER_STEP + i)]
            copies.append(pltpu.make_async_copy(pool_k_ref.at[page_id], kpage_ref.at[into, i], sem.at[into]))
            copies.append(pltpu.make_async_copy(pool_v_ref.at[page_id], vpage_ref.at[into, i], sem.at[into]))
        return copies

    def start_all(copies):
        for j, copy in enumerate(copies):
            copy.start(priority=j % DMA_QUEUES)

    @pl.when((b == 0) & (g == 0))
    def _():
        start_all(page_copies(b, g, slot))

    @pl.when((b < n_seq - 1) | (g < n_steps - 1))
    def _():
        wrap = g == n_steps - 1
        start_all(page_copies(jnp.where(wrap, b + 1, b), jnp.where(wrap, 0, g + 1), 1 - slot))

    for copy in page_copies(b, g, slot):
        copy.wait()

    @pl.when(g == 0)
    def _():
        qidx = lax.broadcasted_iota(jnp.int32, (rows, page), 0) // N_HEADS
        kidx = lax.broadcasted_iota(jnp.int32, (rows, page), 1)
        valid = kidx < qidx
        z = _dot(qbd, knew_ref[0].astype(BF16)) + bias
        right = _sb_suffix(z, tri, valid)
        acc_ref[...] = _dot_nt(_sb_weights(z, right, valid), vnew_ref[0].astype(BF16))
        used_ref[...] = right[:, 0:1]

    for i in range(PAGES_PER_STEP):
        kt_ref[:, i * page:(i + 1) * page] = kpage_ref[slot, i].astype(BF16)
        vt_ref[:, i * page:(i + 1) * page] = vpage_ref[slot, i].astype(BF16)
    z_wide = _dot(qbd, kt_ref[...])
    z = jnp.concatenate([z_wide[:, i * page:(i + 1) * page] + bias for i in range(PAGES_PER_STEP)], axis=0)
    right = _sb_suffix(z, tri, None)
    used = used_ref[...]
    used_rows = []
    for i in range(PAGES_PER_STEP):
        used_rows.append(used)
        used = used + right[i * rows:(i + 1) * rows, 0:1]
    used_ref[...] = used
    p = _sb_weights(z, right + jnp.concatenate(used_rows, axis=0), None)
    p_wide = jnp.concatenate([p[i * rows:(i + 1) * rows] for i in range(PAGES_PER_STEP)], axis=1)
    acc_ref[...] += _dot_nt(p_wide, vt_ref[...])

    @pl.when(g == n_steps - 1)
    def _():
        head = lax.broadcasted_iota(jnp.int32, (rows, d_att), 0) % N_HEADS
        lane_head = lax.broadcasted_iota(jnp.int32, (rows, d_att), 1) // head_dim
        own = jnp.where(head == lane_head, acc_ref[...], 0.0)
        o_ref[0] = jnp.sum(own.reshape(t_new, N_HEADS, d_att), axis=1)


def _sb_attention_sample(q, kt_new, vt_new, cache_k, cache_v, page_table, bias):
    nb, t_new, d_att = q.shape
    head_dim = d_att // N_HEADS
    page = cache_k.shape[2]
    n_pages = page_table.shape[1]
    n_steps = n_pages // PAGES_PER_STEP
    rows = t_new * N_HEADS
    head_mask = (jnp.arange(d_att)[None, :] // head_dim == jnp.arange(N_HEADS)[:, None]).astype(q.dtype)
    qbd = (q[:, :, None, :] * head_mask[None, None]).reshape(nb, rows, d_att)
    bias_rows = jnp.broadcast_to(jnp.tile(bias.astype(F32), t_new)[:, None], (rows, page))
    new_keys = lambda a: jnp.pad(a, ((0, 0), (0, 0), (0, page - t_new)))

    assert n_steps % 2 == 0, "the page double buffer alternates by step parity"
    seq_spec = lambda r, c: pl.BlockSpec((1, r, c), lambda b, g, pt: (b, 0, 0))
    pool_spec = pl.BlockSpec(memory_space=pl.ANY)
    grid_spec = pltpu.PrefetchScalarGridSpec(
        num_scalar_prefetch=1,
        grid=(nb, n_steps),
        in_specs=[seq_spec(rows, d_att), pl.BlockSpec((rows, page), lambda b, g, pt: (0, 0)),
                  seq_spec(d_att, page), seq_spec(d_att, page), pool_spec, pool_spec],
        out_specs=seq_spec(t_new, d_att),
        scratch_shapes=[pltpu.VMEM((2, PAGES_PER_STEP, d_att, page), F32),
                        pltpu.VMEM((2, PAGES_PER_STEP, d_att, page), F32),
                        pltpu.SemaphoreType.DMA((2,)),
                        pltpu.VMEM((d_att, PAGES_PER_STEP * page), BF16),
                        pltpu.VMEM((d_att, PAGES_PER_STEP * page), BF16),
                        pltpu.VMEM((rows, d_att), F32), pltpu.VMEM((rows, 1), F32)])
    return pl.pallas_call(
        functools.partial(_sb_sample_body, page=page, t_new=t_new, n_steps=n_steps),
        grid_spec=grid_spec,
        out_shape=jax.ShapeDtypeStruct((nb, t_new, d_att), F32),
        compiler_params=_params(2),
        name="sb_sample",
    )(page_table, qbd, bias_rows, new_keys(kt_new), new_keys(vt_new), cache_k, cache_v)


def _conv_body(prev_ref, cur_ref, w_ref, b_ref, o_ref, ext_ref, *, width, tiles_per_seq, chunk):
    bs, tm, _ = cur_ref.shape
    span = ext_ref.shape[2]
    prev = prev_ref[...]
    if tiles_per_seq is not None:
        prev = jnp.where(pl.program_id(0) % tiles_per_seq == 0, 0.0, prev)
    ext_ref[0, :, 0:HALO, :] = prev
    ext_ref[0, :, HALO:HALO + tm, :] = cur_ref[...]
    for s in range(1, SUBLANES):
        ext_ref[s, :, 0:span - SUBLANES, :] = ext_ref[0, :, s:s + span - SUBLANES, :]
    first = HALO - (width - 1)
    for b in range(bs):
        for r0 in range(0, tm, chunk):
            acc = None
            for w in range(width):
                shift = (first + w) % SUBLANES
                base = first + w - shift + r0
                term = ext_ref[shift, b, base:base + chunk, :] * w_ref[w:w + 1, :]
                acc = term if acc is None else acc + term
            o_ref[b, r0:r0 + chunk, :] = acc + b_ref[...]


def _depthwise_conv(prev, cur, w_dw, b_dw, *, bs, tm, tiles_per_seq):
    n, rows, c = cur.shape
    width = w_dw.shape[0]
    w_pad = jnp.pad(w_dw, ((0, HALO - width), (0, 0)))
    if tiles_per_seq is None:
        grid = (n // bs,)
        prev_index = lambda i: (i, 0, 0)
        cur_index = lambda i: (i, 0, 0)
    else:
        grid = (n * tiles_per_seq,)
        halo_per_tile = tm // HALO
        prev_index = lambda i: (jnp.maximum(i * halo_per_tile - 1, 0), 0, 0)
        cur_index = lambda i: (i // tiles_per_seq, i % tiles_per_seq, 0)
    return pl.pallas_call(
        functools.partial(_conv_body, width=width, tiles_per_seq=tiles_per_seq, chunk=min(64, tm)),
        grid=grid,
        in_specs=[pl.BlockSpec((bs, HALO, c), prev_index),
                  pl.BlockSpec((bs, tm, c), cur_index),
                  pl.BlockSpec((HALO, c), lambda i: (0, 0)),
                  pl.BlockSpec((1, c), lambda i: (0, 0))],
        out_specs=pl.BlockSpec((bs, tm, c), cur_index),
        out_shape=jax.ShapeDtypeStruct((n, rows, c), F32),
        scratch_shapes=[pltpu.VMEM((SUBLANES, bs, HALO + tm, c), F32)],
        compiler_params=_params(1),
        name="conv",
    )(prev, cur, w_pad, b_dw.reshape(1, c))


def _merge_body(att_ref, dw_ref, ga_ref, gc_ref, x_ref, gt1_ref, sc2_ref, sh2_ref,
                w_att_ref, ln_g_ref, ln_b_ref, w_conv_ref, b_conv_ref, w_out_ref,
                g_post_ref, g_pre_ref, w_router_ref, b_router_ref,
                x1_ref, h2_ref, gate_ref, expert_ref):
    y_att = _dot(att_ref[...].astype(BF16), w_att_ref[...])
    dw = dw_ref[...]
    mu = jnp.mean(dw, axis=-1, keepdims=True)
    xc = dw - mu
    ln = xc * lax.rsqrt(jnp.mean(xc * xc, axis=-1, keepdims=True) + LN_EPS) * ln_g_ref[...] + ln_b_ref[...]
    act = ln * _sigmoid(ln)
    y_conv = _dot(act.astype(BF16), w_conv_ref[...]) + b_conv_ref[...]
    mixed = ga_ref[...] * y_att + gc_ref[...] * y_conv
    out = _dot(mixed.astype(BF16), w_out_ref[...])
    x1 = x_ref[...] + gt1_ref[...] * _rms(out, g_post_ref[...])
    x1_ref[...] = x1
    h2 = _rms(x1, g_pre_ref[...]) * (1.0 + sc2_ref[...]) + sh2_ref[...]
    h2_ref[...] = h2
    logits = _dot(h2.astype(BF16), w_router_ref[...]) + b_router_ref[...]

    n_exp = logits.shape[1]
    lane = lax.broadcasted_iota(jnp.int32, logits.shape, 1)
    out_lane = lax.broadcasted_iota(jnp.int32, gate_ref.shape, 1)
    experts = jnp.zeros(expert_ref.shape, jnp.int32)
    weights = jnp.zeros(gate_ref.shape, F32)
    top = None
    for k in range(TOP_K):
        best = jnp.max(logits, axis=-1, keepdims=True)
        idx = jnp.min(jnp.where(logits == best, lane, n_exp), axis=-1, keepdims=True)
        top = best if top is None else top
        experts = jnp.where(out_lane == k, idx, experts)
        weights = jnp.where(out_lane == k, jnp.exp(best - top), weights)
        logits = jnp.where(lane == idx, -jnp.inf, logits)
    gate_ref[...] = weights / jnp.sum(weights, axis=-1, keepdims=True)
    expert_ref[...] = experts


def _merge(att, dw, ga, gc, x, gt1, sc2, sh2, weights, *, per_token, rows_per_seq):
    rows, d = x.shape
    tm = min(TOKEN_TILE, rows)
    row_spec = lambda a: pl.BlockSpec((tm, a.shape[1]), lambda i: (i, 0))
    mod_spec = _mod_spec(per_token, tm, d, rows_per_seq)
    full_spec = lambda a: pl.BlockSpec(a.shape, lambda i: (0, 0))
    out_widths = (d, d, LANES, LANES)
    out_dtypes = (F32, F32, F32, jnp.int32)
    return pl.pallas_call(
        _merge_body,
        grid=(rows // tm,),
        in_specs=[row_spec(a) for a in (att, dw, ga, gc, x)] + [mod_spec] * 3 + [full_spec(w) for w in weights],
        out_specs=[pl.BlockSpec((tm, w), lambda i: (i, 0)) for w in out_widths],
        out_shape=[jax.ShapeDtypeStruct((rows, w), t) for w, t in zip(out_widths, out_dtypes)],
        compiler_params=_params(1),
        name="merge",
    )(att, dw, ga, gc, x, gt1, sc2, sh2, *weights)


def _to_bf16_body(w_ref, o_ref):
    o_ref[...] = w_ref[...].astype(BF16)


def _to_bf16(w):
    n_exp, rows, cols = w.shape
    tr = min(512, rows)
    spec = pl.BlockSpec((None, tr, cols), lambda e, r: (e, r, 0))
    return pl.pallas_call(
        _to_bf16_body,
        grid=(n_exp, rows // tr),
        in_specs=[spec],
        out_specs=spec,
        out_shape=jax.ShapeDtypeStruct(w.shape, BF16),
        compiler_params=_params(2),
        name="to_bf16",
    )(w)


def _moe_body(tile_expert_ref, n_tiles_ref, x_ref, w_gu_ref, b_gu_ref, w_down_ref, b_down_ref, o_ref, *, chunk):
    d_expert = w_down_ref.shape[0]

    @pl.when(pl.program_id(0) < n_tiles_ref[0])
    def _():
        x = x_ref[...].astype(BF16)
        acc = None
        for c0 in range(0, d_expert, chunk):
            g = _dot(x, w_gu_ref[:, c0:c0 + chunk]) + b_gu_ref[:, c0:c0 + chunk]
            u = _dot(x, w_gu_ref[:, d_expert + c0:d_expert + c0 + chunk]) + b_gu_ref[:, d_expert + c0:d_expert + c0 + chunk]
            g = jnp.minimum(g, SWIGLU_LIMIT)
            u = jnp.clip(u, -SWIGLU_LIMIT, SWIGLU_LIMIT)
            a = ((u + 1.0) * g * _sigmoid(SWIGLU_ALPHA * g)).astype(BF16)
            part = _dot(a, w_down_ref[c0:c0 + chunk, :])
            acc = part if acc is None else acc + part
        o_ref[...] = acc + b_down_ref[...]

    @pl.when(pl.program_id(0) >= n_tiles_ref[0])
    def _():
        o_ref[...] = jnp.zeros_like(o_ref)


def _moe_experts(x_sorted, tile_expert, n_tiles, w_gu, b_gu, w_down, b_down):
    n_rows, d = x_sorted.shape
    n_exp, _, d_gu = w_gu.shape
    d_expert = w_down.shape[1]
    max_tiles = n_rows // MOE_TILE

    def row_index(i, te, nt):
        return (jnp.minimum(i, nt[0] - 1), 0)

    def expert_index(i, te, nt):
        return (te[i], 0, 0)

    grid_spec = pltpu.PrefetchScalarGridSpec(
        num_scalar_prefetch=2,
        grid=(max_tiles,),
        in_specs=[pl.BlockSpec((MOE_TILE, d), row_index),
                  pl.BlockSpec((None, d, d_gu), expert_index),
                  pl.BlockSpec((None, 1, d_gu), expert_index),
                  pl.BlockSpec((None, d_expert, d), expert_index),
                  pl.BlockSpec((None, 1, d), expert_index)],
        out_specs=pl.BlockSpec((MOE_TILE, d), lambda i, te, nt: (i, 0)))
    return pl.pallas_call(
        functools.partial(_moe_body, chunk=min(512, d_expert)),
        grid_spec=grid_spec,
        out_shape=jax.ShapeDtypeStruct((n_rows, d), F32),
        compiler_params=_params(1),
        name="moe",
    )(tile_expert, n_tiles, x_sorted, w_gu, b_gu.reshape(n_exp, 1, d_gu), w_down, b_down.reshape(n_exp, 1, d))


def _rank_body(expert_ref, rank_ref, count_ref, carry_ref):
    i = pl.program_id(0)
    tm, lanes = expert_ref.shape

    @pl.when(i == 0)
    def _():
        carry_ref[...] = jnp.zeros_like(carry_ref)

    experts = expert_ref[...]
    lane = lax.broadcasted_iota(jnp.int32, (tm, lanes), 1)
    picks = [lane == experts[:, k:k + 1] for k in range(TOP_K)]
    chosen = functools.reduce(jnp.logical_or, picks)
    row = lax.broadcasted_iota(jnp.int32, (tm, tm), 0)
    col = lax.broadcasted_iota(jnp.int32, (tm, tm), 1)
    earlier = (col < row).astype(BF16)
    before = _dot(earlier, jnp.where(chosen, 1.0, 0.0).astype(BF16)) + carry_ref[...]
    rank = jnp.zeros((tm, lanes), F32)
    for k in range(TOP_K):
        rank = jnp.where(lane == k, jnp.sum(jnp.where(picks[k], before, 0.0), axis=-1, keepdims=True), rank)
    rank_ref[...] = rank.astype(jnp.int32)
    carry_ref[...] += jnp.sum(jnp.where(chosen, 1.0, 0.0), axis=0, keepdims=True)
    count_ref[...] = carry_ref[...].astype(jnp.int32)


def _rank(experts):
    n_tok, lanes = experts.shape
    tm = TOKEN_TILE
    return pl.pallas_call(
        _rank_body,
        grid=(n_tok // tm,),
        in_specs=[pl.BlockSpec((tm, lanes), lambda i: (i, 0))],
        out_specs=[pl.BlockSpec((tm, lanes), lambda i: (i, 0)), pl.BlockSpec((1, lanes), lambda i: (0, 0))],
        out_shape=[jax.ShapeDtypeStruct((n_tok, lanes), jnp.int32), jax.ShapeDtypeStruct((1, lanes), jnp.int32)],
        scratch_shapes=[pltpu.VMEM((1, lanes), F32)],
        compiler_params=_params(1),
        name="rank",
    )(experts)


def _layout(counts, n_assign):
    n_exp = counts.shape[0]
    padded = (counts + MOE_TILE - 1) // MOE_TILE * MOE_TILE
    pend = jnp.cumsum(padded)
    max_tiles = (n_assign + MOE_TILE - 1) // MOE_TILE + n_exp
    n_tiles = (pend[-1] // MOE_TILE).astype(jnp.int32)
    tile_id = jnp.minimum(jnp.arange(max_tiles, dtype=jnp.int32), n_tiles - 1)
    tile_expert = jnp.sum((pend[None, :] <= (tile_id * MOE_TILE)[:, None]).astype(jnp.int32), axis=1)
    return pend - padded, jnp.minimum(tile_expert, n_exp - 1), n_tiles.reshape(1), max_tiles * MOE_TILE


def _row_copy(src_ref, src_row, dst_ref, dst_row, sem):
    return pltpu.make_async_copy(src_ref.at[pl.ds(src_row, 1)], dst_ref.at[pl.ds(dst_row, 1)], sem)


def _dispatch_body(pos_ref, h_ref, init_ref, o_ref, sem):
    del init_ref
    tm = h_ref.shape[0]

    def issue(t, carry):
        for k in range(TOP_K):
            _row_copy(h_ref, t, o_ref, pos_ref[0, 0, t * TOP_K + k], sem).start()
        return carry

    lax.fori_loop(0, tm, issue, 0)

    for k in range(TOP_K):
        pltpu.make_async_copy(h_ref, o_ref.at[pl.ds(0, tm)], sem).wait()


def _dispatch(h, pos, buffer):
    n_tok, d = h.shape
    tm = min(TOKEN_TILE, n_tok)
    return pl.pallas_call(
        _dispatch_body,
        grid=(n_tok // tm,),
        in_specs=[pl.BlockSpec((1, 1, tm * TOP_K), lambda i: (i, 0, 0), memory_space=pltpu.SMEM),
                  pl.BlockSpec((tm, d), lambda i: (i, 0)),
                  pl.BlockSpec(memory_space=pl.ANY)],
        out_specs=pl.BlockSpec(memory_space=pl.ANY),
        out_shape=jax.ShapeDtypeStruct(buffer.shape, buffer.dtype),
        scratch_shapes=[pltpu.SemaphoreType.DMA(())],
        input_output_aliases={2: 0},
        compiler_params=_params(1),
        name="dispatch",
    )(pos.reshape(n_tok // tm, 1, tm * TOP_K), h, buffer)


def _combine_body(pos_ref, y_ref, gate_ref, x1_ref, gt2_ref, g_ref, o_ref, buf_ref, sem):
    tm = x1_ref.shape[0]

    def issue(t, carry):
        for k in range(TOP_K):
            _row_copy(y_ref, pos_ref[0, 0, t * TOP_K + k], buf_ref.at[k], t, sem).start()
        return carry

    lax.fori_loop(0, tm, issue, 0)

    for k in range(TOP_K):
        pltpu.make_async_copy(y_ref.at[pl.ds(0, tm)], buf_ref.at[k], sem).wait()
    gates = gate_ref[...]
    y = gates[:, 0:1] * buf_ref[0]
    for k in range(1, TOP_K):
        y += gates[:, k:k + 1] * buf_ref[k]
    o_ref[...] = x1_ref[...] + gt2_ref[...] * _rms(y, g_ref[...])


def _combine(y_sorted, pos, gates, x1, gt2, g, *, per_token, rows_per_seq):
    rows, d = x1.shape
    tm = min(TOKEN_TILE, rows)
    row_spec = lambda w: pl.BlockSpec((tm, w), lambda i: (i, 0))
    return pl.pallas_call(
        _combine_body,
        grid=(rows // tm,),
        in_specs=[pl.BlockSpec((1, 1, tm * TOP_K), lambda i: (i, 0, 0), memory_space=pltpu.SMEM),
                  pl.BlockSpec(memory_space=pl.ANY),
                  row_spec(gates.shape[1]), row_spec(d), _mod_spec(per_token, tm, d, rows_per_seq),
                  pl.BlockSpec((1, d), lambda i: (0, 0))],
        out_specs=row_spec(d),
        out_shape=jax.ShapeDtypeStruct((rows, d), F32),
        scratch_shapes=[pltpu.VMEM((TOP_K, tm, d), F32), pltpu.SemaphoreType.DMA(())],
        compiler_params=_params(1),
        name="combine",
    )(pos.reshape(rows // tm, 1, tm * TOP_K), y_sorted, gates, x1, gt2, g.reshape(1, d))


def kernel(x_prompt, x_sample, cache_k, cache_v, state_conv, page_table, c_prompt, c_sample, w_mod, b_mod, g_pre_mix, g_post_mix, w_in, b_sb, w_att_out, w_dw, b_dw, ln_conv_g, ln_conv_b, w_conv_out, b_conv_out, w_out, g_pre_ffn, g_post_ffn, w_router, b_router, w_gu, b_gu, w_down, b_down):
    depth = w_mod.shape[0]
    assert depth == 1, "single-layer trunk"
    n, s, d = x_prompt.shape
    nb, t_new, _ = x_sample.shape
    d_att = w_att_out.shape[1]
    d_conv = w_dw.shape[2]
    width = w_dw.shape[1]
    n_exp = w_router.shape[2]
    head_dim = d_att // N_HEADS
    page = cache_k.shape[2]
    assert width - 1 <= HALO and s % TOKEN_TILE == 0
    l = 0

    row2 = lambda a: a.reshape(1, -1)
    w_in_b = w_in[l].astype(BF16)
    w_kvt = w_in_b[:, d_att:3 * d_att].T
    merge_w = (w_att_out[l].astype(BF16), row2(ln_conv_g[l]), row2(ln_conv_b[l]),
               w_conv_out[l].astype(BF16), row2(b_conv_out[l]), w_out[l].astype(BF16),
               row2(g_post_mix[l]), row2(g_pre_ffn[l]), w_router[l].astype(BF16), row2(b_router[l]))

    n_seq = n + nb
    c_all = jnp.pad(jnp.concatenate([c_prompt, c_sample], axis=0), ((0, -n_seq % 8), (0, 0)))
    mod = _modulation(c_all, w_mod[l], b_mod[l])
    mod_p = [m.reshape(n, 1, d) for m in jnp.split(mod[:n], 6, axis=-1)]
    mod_s = jnp.split(jnp.repeat(mod[n:n_seq], t_new, axis=0), 6, axis=-1)

    def mixing(x, mods, attend, conv, per_token, rows_per_seq, kv_group):
        sh1, sc1, gt1, sh2, sc2, _ = mods
        q, kt, vt, kb, vb, u, ga, gc = _inproj(x, sc1, sh1, g_pre_mix[l], w_in_b, w_kvt, d_att=d_att, d_conv=d_conv,
                                               per_token=per_token, rows_per_seq=rows_per_seq, kv_group=kv_group)
        att = attend(q, kt, vt, kb, vb)
        dw = conv(u)
        x1, h2, gates, experts = _merge(att, dw, ga, gc, x, gt1, sc2, sh2, merge_w,
                                        per_token=per_token, rows_per_seq=rows_per_seq)
        return x1, h2, gates, experts, kt, vt, u

    def attend_prompt(q, kt, vt, kb, vb):
        shape = (n, s, d_att)
        return _sb_attention_prompt(q.reshape(shape), kb.reshape(shape), vb.reshape(shape),
                                    b_sb[l].astype(F32)).reshape(n * s, d_att)

    def conv_prompt(u):
        tm = min(512, s)
        dw = _depthwise_conv(u.reshape(n * s // HALO, HALO, d_conv), u.reshape(n, s, d_conv), w_dw[l], b_dw[l],
                             bs=1, tm=tm, tiles_per_seq=s // tm)
        return dw.reshape(n * s, d_conv)

    x1_p, h2_p, gates_p, experts_p, kt_p, vt_p, u_p = mixing(x_prompt.reshape(n * s, d), mod_p, attend_prompt,
                                                             conv_prompt, False, s, s)

    pool_k = cache_k[l].transpose(0, 2, 3, 1).reshape(-1, d_att, page)
    pool_v = cache_v[l].transpose(0, 2, 3, 1).reshape(-1, d_att, page)

    def attend_sample(q, kt, vt, kb, vb):
        per_seq = lambda a: a.reshape(d_att, nb, t_new).transpose(1, 0, 2)
        return _sb_attention_sample(q.reshape(nb, t_new, d_att), per_seq(kt), per_seq(vt), pool_k, pool_v,
                                    page_table, b_sb[l]).reshape(nb * t_new, d_att)

    def conv_sample(u):
        t_pad = -t_new % 8
        cur = jnp.pad(u.reshape(nb, t_new, d_conv), ((0, 0), (0, t_pad), (0, 0)))
        prev = jnp.pad(state_conv[l], ((0, 0), (HALO - (width - 1), 0), (0, 0)))
        dw = _depthwise_conv(prev, cur, w_dw[l], b_dw[l], bs=min(16, nb), tm=t_new + t_pad, tiles_per_seq=None)
        return dw[:, :t_new].reshape(nb * t_new, d_conv)

    x1_s, h2_s, gates_s, experts_s, kt_s, vt_s, u_s = mixing(x_sample.reshape(nb * t_new, d), mod_s, attend_sample,
                                                             conv_sample, True, t_new, nb * t_new)

    n_p = n * s
    n_tok = n_p + nb * t_new
    assert n_p % TOKEN_TILE == 0 and n_tok % TOKEN_TILE == 0
    experts = jnp.concatenate([experts_p, experts_s], axis=0)
    rank, counts = _rank(experts)
    row_start, tile_expert, n_tiles, n_rows = _layout(counts[0, :n_exp], n_tok * TOP_K)
    pos = row_start[experts[:, :TOP_K]] + rank[:, :TOP_K]
    x_sorted = _dispatch(h2_p, pos[:n_p], jnp.zeros((n_rows, d), F32))
    x_sorted = _dispatch(h2_s, pos[n_p:], x_sorted)
    y_sorted = _moe_experts(x_sorted, tile_expert, n_tiles, _to_bf16(w_gu[l]), b_gu[l], _to_bf16(w_down[l]), b_down[l])
    y_prompt = _combine(y_sorted, pos[:n_p], gates_p, x1_p, mod_p[5], g_post_ffn[l], per_token=False, rows_per_seq=s)
    y_sample = _combine(y_sorted, pos[n_p:], gates_s, x1_s, mod_s[5], g_post_ffn[l], per_token=True,
                        rows_per_seq=t_new)

    heads_last = lambda a, seqs, toks: a.reshape(N_HEADS, head_dim, seqs, toks).transpose(2, 3, 0, 1)[None]
    prompt_kv = lambda a: heads_last(a.transpose(1, 0, 2), n, s)
    sample_kv = lambda a: heads_last(a[0], nb, t_new)
    keep = width - 1
    conv_p = u_p.reshape(n, s, d_conv)[:, s - keep:]
    conv_s = jnp.concatenate([state_conv[l], u_s.reshape(nb, t_new, d_conv)], axis=1)[:, -keep:]
    return (y_prompt.reshape(n, s, d), y_sample.reshape(nb, t_new, d),
            prompt_kv(kt_p), prompt_kv(vt_p), conv_p[None],
            sample_kv(kt_s), sample_kv(vt_s), conv_s[None])
```

```python
import functools

import jax
import jax.numpy as jnp
from jax import lax
from jax.experimental import pallas as pl
from jax.experimental.pallas import tpu as pltpu

F32 = jnp.float32
BF16 = jnp.bfloat16

N_HEADS = 8
TOP_K = 4
SWIGLU_LIMIT = 7.0
SWIGLU_ALPHA = 1.702
RMS_EPS = 1e-6
LN_EPS = 1e-5
SOFTPLUS_LINEAR_ABOVE = 30.0

VMEM_LIMIT_BYTES = 48 * 1024 * 1024
HALO = 32
TOKEN_TILE = 256
ATT_KEY_TILE = 256
ATT_QUERY_TILE = 512
MOE_TILE = 512
LANES = 128
SUBLANES = 8
DMA_QUEUES = 2
PAGES_PER_STEP = 16


def _params(n_axes):
    return pltpu.CompilerParams(dimension_semantics=("arbitrary",) * n_axes,
                                vmem_limit_bytes=VMEM_LIMIT_BYTES)


def _dot(a, b):
    return jnp.dot(a, b, preferred_element_type=F32)


def _dot_nt(a, b):
    return lax.dot_general(a, b, (((1,), (1,)), ((), ())), preferred_element_type=F32)


def _rms(x, g):
    return x * lax.rsqrt(jnp.mean(x * x, axis=-1, keepdims=True) + RMS_EPS) * g


def _sigmoid(x):
    return 1.0 / (1.0 + jnp.exp(-x))


def _mod_body(c_ref, w_ref, b_ref, o_ref):
    c = c_ref[...]
    s = (c * _sigmoid(c)).astype(BF16)
    o_ref[...] = _dot(s, w_ref[...].astype(BF16)) + b_ref[...]


def _modulation(c, w_mod, b_mod):
    n, d = c.shape
    n_out = w_mod.shape[1]
    tn = n_out // 6
    return pl.pallas_call(
        _mod_body,
        grid=(n_out // tn,),
        in_specs=[pl.BlockSpec((n, d), lambda j: (0, 0)),
                  pl.BlockSpec((d, tn), lambda j: (0, j)),
                  pl.BlockSpec((1, tn), lambda j: (0, j))],
        out_specs=pl.BlockSpec((n, tn), lambda j: (0, j)),
        out_shape=jax.ShapeDtypeStruct((n, n_out), F32),
        compiler_params=_params(1),
        name="mod",
    )(c, w_mod, b_mod.reshape(1, n_out))


def _inproj_body(x_ref, sc_ref, sh_ref, g_ref, w_ref, w_kvt_ref,
                 q_ref, kt_ref, vt_ref, kb_ref, vb_ref, u_ref, ga_ref, gc_ref, *, d_att, d_conv, q_scale):
    d_model = x_ref.shape[-1]
    h = _rms(x_ref[...], g_ref[...]) * (1.0 + sc_ref[...]) + sh_ref[...]
    hb = h.astype(BF16)

    def proj(lo, width):
        return _dot(hb, w_ref[:, lo:lo + width])

    q_ref[...] = (proj(0, d_att) * q_scale).astype(BF16)
    kb_ref[...] = proj(d_att, d_att).astype(BF16)
    vb_ref[...] = proj(2 * d_att, d_att).astype(BF16)
    kt_ref[...] = _dot_nt(w_kvt_ref[0:d_att, :], hb)
    vt_ref[...] = _dot_nt(w_kvt_ref[d_att:2 * d_att, :], hb)
    ca = proj(3 * d_att, d_conv)
    cb = proj(3 * d_att + d_conv, d_conv)
    u_ref[...] = ca * _sigmoid(cb)
    ga_ref[...] = _sigmoid(proj(3 * d_att + 2 * d_conv, d_model))
    gc_ref[...] = _sigmoid(proj(3 * d_att + 2 * d_conv + d_model, d_model))


def _mod_spec(per_token, tm, d, rows_per_seq):
    if per_token:
        return pl.BlockSpec((tm, d), lambda i: (i, 0))
    tiles_per_seq = rows_per_seq // tm
    return pl.BlockSpec((None, 1, d), lambda i: (i // tiles_per_seq, 0, 0))


def _inproj(x, sc, sh, g, w_in, w_kvt, *, d_att, d_conv, per_token, rows_per_seq, kv_group):
    rows, d = x.shape
    tm = min(TOKEN_TILE, rows)
    head_dim = d_att // N_HEADS
    tiles_per_group = kv_group // tm
    row_spec = lambda width: pl.BlockSpec((tm, width), lambda i: (i, 0))
    row_out = lambda width, dtype: (row_spec(width), jax.ShapeDtypeStruct((rows, width), dtype))
    kvt_out = (pl.BlockSpec((None, d_att, tm), lambda i: (i // tiles_per_group, 0, i % tiles_per_group)),
               jax.ShapeDtypeStruct((rows // kv_group, d_att, kv_group), F32))
    mod_spec = _mod_spec(per_token, tm, d, rows_per_seq)
    outs = (row_out(d_att, BF16), kvt_out, kvt_out, row_out(d_att, BF16), row_out(d_att, BF16),
            row_out(d_conv, F32), row_out(d, F32), row_out(d, F32))
    return pl.pallas_call(
        functools.partial(_inproj_body, d_att=d_att, d_conv=d_conv, q_scale=head_dim ** -0.5),
        grid=(rows // tm,),
        in_specs=[row_spec(d), mod_spec, mod_spec,
                  pl.BlockSpec((1, d), lambda i: (0, 0)),
                  pl.BlockSpec(w_in.shape, lambda i: (0, 0)),
                  pl.BlockSpec(w_kvt.shape, lambda i: (0, 0))],
        out_specs=[spec for spec, _ in outs],
        out_shape=[shape for _, shape in outs],
        compiler_params=_params(1),
        name="inproj",
    )(x, sc, sh, g.reshape(1, d), w_in, w_kvt)


def _suffix_matrix(tile):
    row = lax.broadcasted_iota(jnp.int32, (tile, tile), 0)
    col = lax.broadcasted_iota(jnp.int32, (tile, tile), 1)
    return (row >= col).astype(BF16)


def _sb_suffix(z, tri, valid):
    sp = jnp.where(z > SOFTPLUS_LINEAR_ABOVE, z, jnp.log(1.0 + jnp.exp(z)))
    if valid is not None:
        sp = jnp.where(valid, sp, 0.0)
    return _dot(sp.astype(BF16), tri)


def _sb_weights(z, right, valid):
    p = jnp.exp(z - right)
    if valid is not None:
        p = jnp.where(valid, p, 0.0)
    return p.astype(BF16)


def _sb_prompt_body(bias_ref, q_ref, k_ref, v_ref, o_ref,
                    qa_ref, z_ref, p_ref, acc_ref, ucur_ref, uprev_ref, *, tq, tk, head_dim):
    hp = pl.program_id(1)
    qi = pl.program_id(2)
    pair = 2 * head_dim
    ratio = tq // tk
    last = (qi + 1) * ratio - 1
    lane = lax.broadcasted_iota(jnp.int32, (1, pair), 1)
    row = lax.broadcasted_iota(jnp.int32, (2 * tq, tk), 0)
    col = lax.broadcasted_iota(jnp.int32, (2 * tq, tk), 1)
    q_pos = jnp.where(row >= tq, row - tq, row)
    causal = [col + (ratio - 1 - t) * tk < q_pos for t in range(ratio)]
    tri = _suffix_matrix(tk)
    ones = jnp.ones((tk, pair), BF16)

    q = q_ref[0]
    for h in range(2):
        qa_ref[h * tq:(h + 1) * tq, 0:pair] = jnp.where(
            (lane >= h * head_dim) & (lane < (h + 1) * head_dim), q, jnp.zeros_like(q))
        b = bias_ref[2 * hp + h]
        b1 = b.astype(BF16).astype(F32)
        b2 = (b - b1).astype(BF16).astype(F32)
        b3 = (b - b1 - b2).astype(BF16).astype(F32)
        offset = jnp.where(lane == 0, b1, jnp.where(lane == 1, b2, jnp.where(lane == 2, b3, 0.0)))
        qa_ref[h * tq:(h + 1) * tq, pair:2 * pair] = jnp.broadcast_to(offset, (tq, pair)).astype(BF16)

    def logits(t, slot):
        off = pl.multiple_of(jnp.maximum(last - t, 0) * tk, tk)
        ka = jnp.concatenate([k_ref[0, pl.ds(off, tk), :], ones], axis=1)
        z_ref[slot] = _dot_nt(qa_ref[...], ka)

    def weights(slot, valid):
        right = _sb_suffix(z_ref[slot], tri, valid)
        p_ref[slot] = _sb_weights(z_ref[slot], right, valid)
        return right[:, 0:1]

    def values(t, slot):
        off = pl.multiple_of((last - t) * tk, tk)
        acc_ref[...] += jnp.exp(uprev_ref[...]) * _dot(p_ref[slot], v_ref[0, pl.ds(off, tk), :])

    def stage(t, slot, valid):
        values(t - 1, 1 - slot)
        total = weights(slot, valid)
        logits(t + 1, 1 - slot)
        ucur = ucur_ref[...]
        uprev_ref[...] = ucur
        ucur_ref[...] = ucur - total

    logits(0, 0)
    logits(1, 1)
    ucur_ref[...] = -weights(0, causal[0])
    uprev_ref[...] = jnp.zeros_like(uprev_ref)
    acc_ref[...] = jnp.zeros_like(acc_ref)
    for t in range(1, ratio):
        stage(t, t % 2, causal[t])

    def two_stages(u, carry):
        stage(ratio + 2 * u, 0, None)
        stage(ratio + 2 * u + 1, 1, None)
        return carry

    lax.fori_loop(0, qi * (ratio // 2), two_stages, 0)
    values(last, 1)
    o_ref[0] = jnp.where(lane < head_dim, acc_ref[0:tq], acc_ref[tq:2 * tq]).astype(o_ref.dtype)


def _sb_attention_prompt(q, k, v, bias):
    n, s, d_att = q.shape
    head_dim = d_att // N_HEADS
    tk = ATT_KEY_TILE
    tq = ATT_QUERY_TILE
    assert s % tq == 0 and tq % (2 * tk) == 0
    pair = 2 * head_dim
    grid_spec = pltpu.PrefetchScalarGridSpec(
        num_scalar_prefetch=1,
        grid=(n, N_HEADS // 2, s // tq),
        in_specs=[pl.BlockSpec((1, tq, pair), lambda b, hp, qi, bias: (b, qi, hp)),
                  pl.BlockSpec((1, s, pair), lambda b, hp, qi, bias: (b, 0, hp)),
                  pl.BlockSpec((1, s, pair), lambda b, hp, qi, bias: (b, 0, hp))],
        out_specs=pl.BlockSpec((1, tq, pair), lambda b, hp, qi, bias: (b, qi, hp)),
        scratch_shapes=[pltpu.VMEM((2 * tq, 2 * pair), BF16),
                        pltpu.VMEM((2, 2 * tq, tk), F32),
                        pltpu.VMEM((2, 2 * tq, tk), BF16),
                        pltpu.VMEM((2 * tq, pair), F32),
                        pltpu.VMEM((2 * tq, 1), F32),
                        pltpu.VMEM((2 * tq, 1), F32)])
    return pl.pallas_call(
        functools.partial(_sb_prompt_body, tq=tq, tk=tk, head_dim=head_dim),
        grid_spec=grid_spec,
        out_shape=jax.ShapeDtypeStruct((n, s, d_att), BF16),
        compiler_params=_params(3),
        name="sb_prompt",
    )(bias, q, k, v)


def _sb_sample_body(pt_ref, qbd_ref, bias_ref, knew_ref, vnew_ref, pool_k_ref, pool_v_ref, o_ref,
                    kpage_ref, vpage_ref, sem, kt_ref, vt_ref, acc_ref, used_ref, *, page, t_new, n_steps):
    b = pl.program_id(0)
    g = pl.program_id(1)
    n_seq = pl.num_programs(0)
    n_pages = n_steps * PAGES_PER_STEP
    rows, d_att = acc_ref.shape
    head_dim = d_att // N_HEADS
    tri = _suffix_matrix(page)
    qbd = qbd_ref[0]
    bias = bias_ref[...]
    slot = g % 2

    def page_copies(seq, step, into):
        copies = []
        for i in range(PAGES_PER_STEP):
            page_id = pt_ref[seq, n_pages - 1 - (step * PAGES_PER_STEP + i)]
            copies.append(pltpu.make_async_copy(pool_k_ref.at[page_id], kpage_ref.at[into, i], sem.at[into]))
            copies.append(pltpu.make_async_copy(pool_v_ref.at[page_id], vpage_ref.at[into, i], sem.at[into]))
        return copies

    def start_all(copies):
        for j, copy in enumerate(copies):
            copy.start(priority=j % DMA_QUEUES)

    @pl.when((b == 0) & (g == 0))
    def _():
        start_all(page_copies(b, g, slot))

    @pl.when((b < n_seq - 1) | (g < n_steps - 1))
    def _():
        wrap = g == n_steps - 1
        start_all(page_copies(jnp.where(wrap, b + 1, b), jnp.where(wrap, 0, g + 1), 1 - slot))

    for copy in page_copies(b, g, slot):
        copy.wait()

    @pl.when(g == 0)
    def _():
        qidx = lax.broadcasted_iota(jnp.int32, (rows, page), 0) // N_HEADS
        kidx = lax.broadcasted_iota(jnp.int32, (rows, page), 1)
        valid = kidx < qidx
        z = _dot(qbd, knew_ref[0].astype(BF16)) + bias
        right = _sb_suffix(z, tri, valid)
        acc_ref[...] = _dot_nt(_sb_weights(z, right, valid), vnew_ref[0].astype(BF16))
        used_ref[...] = right[:, 0:1]

    for i in range(PAGES_PER_STEP):
        kt_ref[:, i * page:(i + 1) * page] = kpage_ref[slot, i].astype(BF16)
        vt_ref[:, i * page:(i + 1) * page] = vpage_ref[slot, i].astype(BF16)
    z_wide = _dot(qbd, kt_ref[...])
    z = jnp.concatenate([z_wide[:, i * page:(i + 1) * page] + bias for i in range(PAGES_PER_STEP)], axis=0)
    right = _sb_suffix(z, tri, None)
    used = used_ref[...]
    used_rows = []
    for i in range(PAGES_PER_STEP):
        used_rows.append(used)
        used = used + right[i * rows:(i + 1) * rows, 0:1]
    used_ref[...] = used
    p = _sb_weights(z, right + jnp.concatenate(used_rows, axis=0), None)
    p_wide = jnp.concatenate([p[i * rows:(i + 1) * rows] for i in range(PAGES_PER_STEP)], axis=1)
    acc_ref[...] += _dot_nt(p_wide, vt_ref[...])

    @pl.when(g == n_steps - 1)
    def _():
        head = lax.broadcasted_iota(jnp.int32, (rows, d_att), 0) % N_HEADS
        lane_head = lax.broadcasted_iota(jnp.int32, (rows, d_att), 1) // head_dim
        own = jnp.where(head == lane_head, acc_ref[...], 0.0)
        o_ref[0] = jnp.sum(own.reshape(t_new, N_HEADS, d_att), axis=1)


def _sb_attention_sample(q, kt_new, vt_new, cache_k, cache_v, page_table, bias):
    nb, t_new, d_att = q.shape
    head_dim = d_att // N_HEADS
    page = cache_k.shape[2]
    n_pages = page_table.shape[1]
    n_steps = n_pages // PAGES_PER_STEP
    rows = t_new * N_HEADS
    head_mask = (jnp.arange(d_att)[None, :] // head_dim == jnp.arange(N_HEADS)[:, None]).astype(q.dtype)
    qbd = (q[:, :, None, :] * head_mask[None, None]).reshape(nb, rows, d_att)
    bias_rows = jnp.broadcast_to(jnp.tile(bias.astype(F32), t_new)[:, None], (rows, page))
    new_keys = lambda a: jnp.pad(a, ((0, 0), (0, 0), (0, page - t_new)))

    assert n_steps % 2 == 0, "the page double buffer alternates by step parity"
    seq_spec = lambda r, c: pl.BlockSpec((1, r, c), lambda b, g, pt: (b, 0, 0))
    pool_spec = pl.BlockSpec(memory_space=pl.ANY)
    grid_spec = pltpu.PrefetchScalarGridSpec(
        num_scalar_prefetch=1,
        grid=(nb, n_steps),
        in_specs=[seq_spec(rows, d_att), pl.BlockSpec((rows, page), lambda b, g, pt: (0, 0)),
                  seq_spec(d_att, page), seq_spec(d_att, page), pool_spec, pool_spec],
        out_specs=seq_spec(t_new, d_att),
        scratch_shapes=[pltpu.VMEM((2, PAGES_PER_STEP, d_att, page), F32),
                        pltpu.VMEM((2, PAGES_PER_STEP, d_att, page), F32),
                        pltpu.SemaphoreType.DMA((2,)),
                        pltpu.VMEM((d_att, PAGES_PER_STEP * page), BF16),
                        pltpu.VMEM((d_att, PAGES_PER_STEP * page), BF16),
                        pltpu.VMEM((rows, d_att), F32), pltpu.VMEM((rows, 1), F32)])
    return pl.pallas_call(
        functools.partial(_sb_sample_body, page=page, t_new=t_new, n_steps=n_steps),
        grid_spec=grid_spec,
        out_shape=jax.ShapeDtypeStruct((nb, t_new, d_att), F32),
        compiler_params=_params(2),
        name="sb_sample",
    )(page_table, qbd, bias_rows, new_keys(kt_new), new_keys(vt_new), cache_k, cache_v)


def _conv_body(prev_ref, cur_ref, w_ref, b_ref, o_ref, ext_ref, *, width, tiles_per_seq, chunk):
    bs, tm, _ = cur_ref.shape
    span = ext_ref.shape[2]
    prev = prev_ref[...]
    if tiles_per_seq is not None:
        prev = jnp.where(pl.program_id(0) % tiles_per_seq == 0, 0.0, prev)
    ext_ref[0, :, 0:HALO, :] = prev
    ext_ref[0, :, HALO:HALO + tm, :] = cur_ref[...]
    for s in range(1, SUBLANES):
        ext_ref[s, :, 0:span - SUBLANES, :] = ext_ref[0, :, s:s + span - SUBLANES, :]
    first = HALO - (width - 1)
    for b in range(bs):
        for r0 in range(0, tm, chunk):
            acc = None
            for w in range(width):
                shift = (first + w) % SUBLANES
                base = first + w - shift + r0
                term = ext_ref[shift, b, base:base + chunk, :] * w_ref[w:w + 1, :]
                acc = term if acc is None else acc + term
            o_ref[b, r0:r0 + chunk, :] = acc + b_ref[...]


def _depthwise_conv(prev, cur, w_dw, b_dw, *, bs, tm, tiles_per_seq):
    n, rows, c = cur.shape
    width = w_dw.shape[0]
    w_pad = jnp.pad(w_dw, ((0, HALO - width), (0, 0)))
    if tiles_per_seq is None:
        grid = (n // bs,)
        prev_index = lambda i: (i, 0, 0)
        cur_index = lambda i: (i, 0, 0)
    else:
        grid = (n * tiles_per_seq,)
        halo_per_tile = tm // HALO
        prev_index = lambda i: (jnp.maximum(i * halo_per_tile - 1, 0), 0, 0)
        cur_index = lambda i: (i // tiles_per_seq, i % tiles_per_seq, 0)
    return pl.pallas_call(
        functools.partial(_conv_body, width=width, tiles_per_seq=tiles_per_seq, chunk=min(64, tm)),
        grid=grid,
        in_specs=[pl.BlockSpec((bs, HALO, c), prev_index),
                  pl.BlockSpec((bs, tm, c), cur_index),
                  pl.BlockSpec((HALO, c), lambda i: (0, 0)),
                  pl.BlockSpec((1, c), lambda i: (0, 0))],
        out_specs=pl.BlockSpec((bs, tm, c), cur_index),
        out_shape=jax.ShapeDtypeStruct((n, rows, c), F32),
        scratch_shapes=[pltpu.VMEM((SUBLANES, bs, HALO + tm, c), F32)],
        compiler_params=_params(1),
        name="conv",
    )(prev, cur, w_pad, b_dw.reshape(1, c))


def _merge_body(att_ref, dw_ref, ga_ref, gc_ref, x_ref, gt1_ref, sc2_ref, sh2_ref,
                w_att_ref, ln_g_ref, ln_b_ref, w_conv_ref, b_conv_ref, w_out_ref,
                g_post_ref, g_pre_ref, w_router_ref, b_router_ref,
                x1_ref, h2_ref, gate_ref, expert_ref):
    y_att = _dot(att_ref[...].astype(BF16), w_att_ref[...])
    dw = dw_ref[...]
    mu = jnp.mean(dw, axis=-1, keepdims=True)
    xc = dw - mu
    ln = xc * lax.rsqrt(jnp.mean(xc * xc, axis=-1, keepdims=True) + LN_EPS) * ln_g_ref[...] + ln_b_ref[...]
    act = ln * _sigmoid(ln)
    y_conv = _dot(act.astype(BF16), w_conv_ref[...]) + b_conv_ref[...]
    mixed = ga_ref[...] * y_att + gc_ref[...] * y_conv
    out = _dot(mixed.astype(BF16), w_out_ref[...])
    x1 = x_ref[...] + gt1_ref[...] * _rms(out, g_post_ref[...])
    x1_ref[...] = x1
    h2 = _rms(x1, g_pre_ref[...]) * (1.0 + sc2_ref[...]) + sh2_ref[...]
    h2_ref[...] = h2
    logits = _dot(h2.astype(BF16), w_router_ref[...]) + b_router_ref[...]

    n_exp = logits.shape[1]
    lane = lax.broadcasted_iota(jnp.int32, logits.shape, 1)
    out_lane = lax.broadcasted_iota(jnp.int32, gate_ref.shape, 1)
    experts = jnp.zeros(expert_ref.shape, jnp.int32)
    weights = jnp.zeros(gate_ref.shape, F32)
    top = None
    for k in range(TOP_K):
        best = jnp.max(logits, axis=-1, keepdims=True)
        idx = jnp.min(jnp.where(logits == best, lane, n_exp), axis=-1, keepdims=True)
        top = best if top is None else top
        experts = jnp.where(out_lane == k, idx, experts)
        weights = jnp.where(out_lane == k, jnp.exp(best - top), weights)
        logits = jnp.where(lane == idx, -jnp.inf, logits)
    gate_ref[...] = weights / jnp.sum(weights, axis=-1, keepdims=True)
    expert_ref[...] = experts


def _merge(att, dw, ga, gc, x, gt1, sc2, sh2, weights, *, per_token, rows_per_seq):
    rows, d = x.shape
    tm = min(TOKEN_TILE, rows)
    row_spec = lambda a: pl.BlockSpec((tm, a.shape[1]), lambda i: (i, 0))
    mod_spec = _mod_spec(per_token, tm, d, rows_per_seq)
    full_spec = lambda a: pl.BlockSpec(a.shape, lambda i: (0, 0))
    out_widths = (d, d, LANES, LANES)
    out_dtypes = (F32, F32, F32, jnp.int32)
    return pl.pallas_call(
        _merge_body,
        grid=(rows // tm,),
        in_specs=[row_spec(a) for a in (att, dw, ga, gc, x)] + [mod_spec] * 3 + [full_spec(w) for w in weights],
        out_specs=[pl.BlockSpec((tm, w), lambda i: (i, 0)) for w in out_widths],
        out_shape=[jax.ShapeDtypeStruct((rows, w), t) for w, t in zip(out_widths, out_dtypes)],
        compiler_params=_params(1),
        name="merge",
    )(att, dw, ga, gc, x, gt1, sc2, sh2, *weights)


def _to_bf16_body(w_ref, o_ref):
    o_ref[...] = w_ref[...].astype(BF16)


def _to_bf16(w):
    n_exp, rows, cols = w.shape
    tr = min(512, rows)
    spec = pl.BlockSpec((None, tr, cols), lambda e, r: (e, r, 0))
    return pl.pallas_call(
        _to_bf16_body,
        grid=(n_exp, rows // tr),
        in_specs=[spec],
        out_specs=spec,
        out_shape=jax.ShapeDtypeStruct(w.shape, BF16),
        compiler_params=_params(2),
        name="to_bf16",
    )(w)


def _moe_body(tile_expert_ref, n_tiles_ref, x_ref, w_gu_ref, b_gu_ref, w_down_ref, b_down_ref, o_ref, *, chunk):
    d_expert = w_down_ref.shape[0]

    @pl.when(pl.program_id(0) < n_tiles_ref[0])
    def _():
        x = x_ref[...].astype(BF16)
        acc = None
        for c0 in range(0, d_expert, chunk):
            g = _dot(x, w_gu_ref[:, c0:c0 + chunk]) + b_gu_ref[:, c0:c0 + chunk]
            u = _dot(x, w_gu_ref[:, d_expert + c0:d_expert + c0 + chunk]) + b_gu_ref[:, d_expert + c0:d_expert + c0 + chunk]
            g = jnp.minimum(g, SWIGLU_LIMIT)
            u = jnp.clip(u, -SWIGLU_LIMIT, SWIGLU_LIMIT)
            a = ((u + 1.0) * g * _sigmoid(SWIGLU_ALPHA * g)).astype(BF16)
            part = _dot(a, w_down_ref[c0:c0 + chunk, :])
            acc = part if acc is None else acc + part
        o_ref[...] = acc + b_down_ref[...]

    @pl.when(pl.program_id(0) >= n_tiles_ref[0])
    def _():
        o_ref[...] = jnp.zeros_like(o_ref)


def _moe_experts(x_sorted, tile_expert, n_tiles, w_gu, b_gu, w_down, b_down):
    n_rows, d = x_sorted.shape
    n_exp, _, d_gu = w_gu.shape
    d_expert = w_down.shape[1]
    max_tiles = n_rows // MOE_TILE

    def row_index(i, te, nt):
        return (jnp.minimum(i, nt[0] - 1), 0)

    def expert_index(i, te, nt):
        return (te[i], 0, 0)

    grid_spec = pltpu.PrefetchScalarGridSpec(
        num_scalar_prefetch=2,
        grid=(max_tiles,),
        in_specs=[pl.BlockSpec((MOE_TILE, d), row_index),
                  pl.BlockSpec((None, d, d_gu), expert_index),
                  pl.BlockSpec((None, 1, d_gu), expert_index),
                  pl.BlockSpec((None, d_expert, d), expert_index),
                  pl.BlockSpec((None, 1, d), expert_index)],
        out_specs=pl.BlockSpec((MOE_TILE, d), lambda i, te, nt: (i, 0)))
    return pl.pallas_call(
        functools.partial(_moe_body, chunk=min(512, d_expert)),
        grid_spec=grid_spec,
        out_shape=jax.ShapeDtypeStruct((n_rows, d), F32),
        compiler_params=_params(1),
        name="moe",
    )(tile_expert, n_tiles, x_sorted, w_gu, b_gu.reshape(n_exp, 1, d_gu), w_down, b_down.reshape(n_exp, 1, d))


def _rank_body(expert_ref, rank_ref, count_ref, carry_ref):
    i = pl.program_id(0)
    tm, lanes = expert_ref.shape

    @pl.when(i == 0)
    def _():
        carry_ref[...] = jnp.zeros_like(carry_ref)

    experts = expert_ref[...]
    lane = lax.broadcasted_iota(jnp.int32, (tm, lanes), 1)
    picks = [lane == experts[:, k:k + 1] for k in range(TOP_K)]
    chosen = functools.reduce(jnp.logical_or, picks)
    row = lax.broadcasted_iota(jnp.int32, (tm, tm), 0)
    col = lax.broadcasted_iota(jnp.int32, (tm, tm), 1)
    earlier = (col < row).astype(BF16)
    before = _dot(earlier, jnp.where(chosen, 1.0, 0.0).astype(BF16)) + carry_ref[...]
    rank = jnp.zeros((tm, lanes), F32)
    for k in range(TOP_K):
        rank = jnp.where(lane == k, jnp.sum(jnp.where(picks[k], before, 0.0), axis=-1, keepdims=True), rank)
    rank_ref[...] = rank.astype(jnp.int32)
    carry_ref[...] += jnp.sum(jnp.where(chosen, 1.0, 0.0), axis=0, keepdims=True)
    count_ref[...] = carry_ref[...].astype(jnp.int32)


def _rank(experts):
    n_tok, lanes = experts.shape
    tm = TOKEN_TILE
    return pl.pallas_call(
        _rank_body,
        grid=(n_tok // tm,),
        in_specs=[pl.BlockSpec((tm, lanes), lambda i: (i, 0))],
        out_specs=[pl.BlockSpec((tm, lanes), lambda i: (i, 0)), pl.BlockSpec((1, lanes), lambda i: (0, 0))],
        out_shape=[jax.ShapeDtypeStruct((n_tok, lanes), jnp.int32), jax.ShapeDtypeStruct((1, lanes), jnp.int32)],
        scratch_shapes=[pltpu.VMEM((1, lanes), F32)],
        compiler_params=_params(1),
        name="rank",
    )(experts)


def _layout(counts, n_assign):
    n_exp = counts.shape[0]
    padded = (counts + MOE_TILE - 1) // MOE_TILE * MOE_TILE
    pend = jnp.cumsum(padded)
    max_tiles = (n_assign + MOE_TILE - 1) // MOE_TILE + n_exp
    n_tiles = (pend[-1] // MOE_TILE).astype(jnp.int32)
    tile_id = jnp.minimum(jnp.arange(max_tiles, dtype=jnp.int32), n_tiles - 1)
    tile_expert = jnp.sum((pend[None, :] <= (tile_id * MOE_TILE)[:, None]).astype(jnp.int32), axis=1)
    return pend - padded, jnp.minimum(tile_expert, n_exp - 1), n_tiles.reshape(1), max_tiles * MOE_TILE


def _row_copy(src_ref, src_row, dst_ref, dst_row, sem):
    return pltpu.make_async_copy(src_ref.at[pl.ds(src_row, 1)], dst_ref.at[pl.ds(dst_row, 1)], sem)


def _dispatch_body(pos_ref, h_ref, init_ref, o_ref, sem):
    del init_ref
    tm = h_ref.shape[0]

    def issue(t, carry):
        for k in range(TOP_K):
            _row_copy(h_ref, t, o_ref, pos_ref[0, 0, t * TOP_K + k], sem).start()
        return carry

    lax.fori_loop(0, tm, issue, 0)

    for k in range(TOP_K):
        pltpu.make_async_copy(h_ref, o_ref.at[pl.ds(0, tm)], sem).wait()


def _dispatch(h, pos, buffer):
    n_tok, d = h.shape
    tm = min(TOKEN_TILE, n_tok)
    return pl.pallas_call(
        _dispatch_body,
        grid=(n_tok // tm,),
        in_specs=[pl.BlockSpec((1, 1, tm * TOP_K), lambda i: (i, 0, 0), memory_space=pltpu.SMEM),
                  pl.BlockSpec((tm, d), lambda i: (i, 0)),
                  pl.BlockSpec(memory_space=pl.ANY)],
        out_specs=pl.BlockSpec(memory_space=pl.ANY),
        out_shape=jax.ShapeDtypeStruct(buffer.shape, buffer.dtype),
        scratch_shapes=[pltpu.SemaphoreType.DMA(())],
        input_output_aliases={2: 0},
        compiler_params=_params(1),
        name="dispatch",
    )(pos.reshape(n_tok // tm, 1, tm * TOP_K), h, buffer)


def _combine_body(pos_ref, y_ref, gate_ref, x1_ref, gt2_ref, g_ref, o_ref, buf_ref, sem):
    tm = x1_ref.shape[0]

    def issue(t, carry):
        for k in range(TOP_K):
            _row_copy(y_ref, pos_ref[0, 0, t * TOP_K + k], buf_ref.at[k], t, sem).start()
        return carry

    lax.fori_loop(0, tm, issue, 0)

    for k in range(TOP_K):
        pltpu.make_async_copy(y_ref.at[pl.ds(0, tm)], buf_ref.at[k], sem).wait()
    gates = gate_ref[...]
    y = gates[:, 0:1] * buf_ref[0]
    for k in range(1, TOP_K):
        y += gates[:, k:k + 1] * buf_ref[k]
    o_ref[...] = x1_ref[...] + gt2_ref[...] * _rms(y, g_ref[...])


def _combine(y_sorted, pos, gates, x1, gt2, g, *, per_token, rows_per_seq):
    rows, d = x1.shape
    tm = min(TOKEN_TILE, rows)
    row_spec = lambda w: pl.BlockSpec((tm, w), lambda i: (i, 0))
    return pl.pallas_call(
        _combine_body,
        grid=(rows // tm,),
        in_specs=[pl.BlockSpec((1, 1, tm * TOP_K), lambda i: (i, 0, 0), memory_space=pltpu.SMEM),
                  pl.BlockSpec(memory_space=pl.ANY),
                  row_spec(gates.shape[1]), row_spec(d), _mod_spec(per_token, tm, d, rows_per_seq),
                  pl.BlockSpec((1, d), lambda i: (0, 0))],
        out_specs=row_spec(d),
        out_shape=jax.ShapeDtypeStruct((rows, d), F32),
        scratch_shapes=[pltpu.VMEM((TOP_K, tm, d), F32), pltpu.SemaphoreType.DMA(())],
        compiler_params=_params(1),
        name="combine",
    )(pos.reshape(rows // tm, 1, tm * TOP_K), y_sorted, gates, x1, gt2, g.reshape(1, d))


def kernel(x_prompt, x_sample, cache_k, cache_v, state_conv, page_table, c_prompt, c_sample, w_mod, b_mod, g_pre_mix, g_post_mix, w_in, b_sb, w_att_out, w_dw, b_dw, ln_conv_g, ln_conv_b, w_conv_out, b_conv_out, w_out, g_pre_ffn, g_post_ffn, w_router, b_router, w_gu, b_gu, w_down, b_down):
    depth = w_mod.shape[0]
    assert depth == 1, "single-layer trunk"
    n, s, d = x_prompt.shape
    nb, t_new, _ = x_sample.shape
    d_att = w_att_out.shape[1]
    d_conv = w_dw.shape[2]
    width = w_dw.shape[1]
    n_exp = w_router.shape[2]
    head_dim = d_att // N_HEADS
    page = cache_k.shape[2]
    assert width - 1 <= HALO and s % TOKEN_TILE == 0
    l = 0

    row2 = lambda a: a.reshape(1, -1)
    w_in_b = w_in[l].astype(BF16)
    w_kvt = w_in_b[:, d_att:3 * d_att].T
    merge_w = (w_att_out[l].astype(BF16), row2(ln_conv_g[l]), row2(ln_conv_b[l]),
               w_conv_out[l].astype(BF16), row2(b_conv_out[l]), w_out[l].astype(BF16),
               row2(g_post_mix[l]), row2(g_pre_ffn[l]), w_router[l].astype(BF16), row2(b_router[l]))

    n_seq = n + nb
    c_all = jnp.pad(jnp.concatenate([c_prompt, c_sample], axis=0), ((0, -n_seq % 8), (0, 0)))
    mod = _modulation(c_all, w_mod[l], b_mod[l])
    mod_p = [m.reshape(n, 1, d) for m in jnp.split(mod[:n], 6, axis=-1)]
    mod_s = jnp.split(jnp.repeat(mod[n:n_seq], t_new, axis=0), 6, axis=-1)

    def mixing(x, mods, attend, conv, per_token, rows_per_seq, kv_group):
        sh1, sc1, gt1, sh2, sc2, _ = mods
        q, kt, vt, kb, vb, u, ga, gc = _inproj(x, sc1, sh1, g_pre_mix[l], w_in_b, w_kvt, d_att=d_att, d_conv=d_conv,
                                               per_token=per_token, rows_per_seq=rows_per_seq, kv_group=kv_group)
        att = attend(q, kt, vt, kb, vb)
        dw = conv(u)
        x1, h2, gates, experts = _merge(att, dw, ga, gc, x, gt1, sc2, sh2, merge_w,
                                        per_token=per_token, rows_per_seq=rows_per_seq)
        return x1, h2, gates, experts, kt, vt, u

    def attend_prompt(q, kt, vt, kb, vb):
        shape = (n, s, d_att)
        return _sb_attention_prompt(q.reshape(shape), kb.reshape(shape), vb.reshape(shape),
                                    b_sb[l].astype(F32)).reshape(n * s, d_att)

    def conv_prompt(u):
        tm = min(512, s)
        dw = _depthwise_conv(u.reshape(n * s // HALO, HALO, d_conv), u.reshape(n, s, d_conv), w_dw[l], b_dw[l],
                             bs=1, tm=tm, tiles_per_seq=s // tm)
        return dw.reshape(n * s, d_conv)

    x1_p, h2_p, gates_p, experts_p, kt_p, vt_p, u_p = mixing(x_prompt.reshape(n * s, d), mod_p, attend_prompt,
                                                             conv_prompt, False, s, s)

    pool_k = cache_k[l].transpose(0, 2, 3, 1).reshape(-1, d_att, page)
    pool_v = cache_v[l].transpose(0, 2, 3, 1).reshape(-1, d_att, page)

    def attend_sample(q, kt, vt, kb, vb):
        per_seq = lambda a: a.reshape(d_att, nb, t_new).transpose(1, 0, 2)
        return _sb_attention_sample(q.reshape(nb, t_new, d_att), per_seq(kt), per_seq(vt), pool_k, pool_v,
                                    page_table, b_sb[l]).reshape(nb * t_new, d_att)

    def conv_sample(u):
        t_pad = -t_new % 8
        cur = jnp.pad(u.reshape(nb, t_new, d_conv), ((0, 0), (0, t_pad), (0, 0)))
        prev = jnp.pad(state_conv[l], ((0, 0), (HALO - (width - 1), 0), (0, 0)))
        dw = _depthwise_conv(prev, cur, w_dw[l], b_dw[l], bs=min(16, nb), tm=t_new + t_pad, tiles_per_seq=None)
        return dw[:, :t_new].reshape(nb * t_new, d_conv)

    x1_s, h2_s, gates_s, experts_s, kt_s, vt_s, u_s = mixing(x_sample.reshape(nb * t_new, d), mod_s, attend_sample,
                                                             conv_sample, True, t_new, nb * t_new)

    n_p = n * s
    n_tok = n_p + nb * t_new
    assert n_p % TOKEN_TILE == 0 and n_tok % TOKEN_TILE == 0
    experts = jnp.concatenate([experts_p, experts_s], axis=0)
    rank, counts = _rank(experts)
    row_start, tile_expert, n_tiles, n_rows = _layout(counts[0, :n_exp], n_tok * TOP_K)
    pos = row_start[experts[:, :TOP_K]] + rank[:, :TOP_K]
    x_sorted = _dispatch(h2_p, pos[:n_p], jnp.zeros((n_rows, d), F32))
    x_sorted = _dispatch(h2_s, pos[n_p:], x_sorted)
    y_sorted = _moe_experts(x_sorted, tile_expert, n_tiles, _to_bf16(w_gu[l]), b_gu[l], _to_bf16(w_down[l]), b_down[l])
    y_prompt = _combine(y_sorted, pos[:n_p], gates_p, x1_p, mod_p[5], g_post_ffn[l], per_token=False, rows_per_seq=s)
    y_sample = _combine(y_sorted, pos[n_p:], gates_s, x1_s, mod_s[5], g_post_ffn[l], per_token=True,
                        rows_per_seq=t_new)

    heads_last = lambda a, seqs, toks: a.reshape(N_HEADS, head_dim, seqs, toks).transpose(2, 3, 0, 1)[None]
    prompt_kv = lambda a: heads_last(a.transpose(1, 0, 2), n, s)
    sample_kv = lambda a: heads_last(a[0], nb, t_new)
    keep = width - 1
    conv_p = u_p.reshape(n, s, d_conv)[:, s - keep:]
    conv_s = jnp.concatenate([state_conv[l], u_s.reshape(nb, t_new, d_conv)], axis=1)[:, -keep:]
    return (y_prompt.reshape(n, s, d), y_sample.reshape(nb, t_new, d),
            prompt_kv(kt_p), prompt_kv(vt_p), conv_p[None],
            sample_kv(kt_s), sample_kv(vt_s), conv_s[None])
```
